```python
import jax, jax.numpy as jnp
from jax import lax
import numpy as np

D_MODEL = 2048
BATCH = 2
SEQ = 4096
DEPTH = 4

GRID_W = 64
CTX_LEN = 256
N_MIXERS = 3
MIXER_POOL = 0
MIXER_CONV = 1
MIXER_RGLRU = 2
POOL_WINDOWS = (2, 4, 8, 16)
POOL_GROUP = D_MODEL // len(POOL_WINDOWS)
CONV_WIDTH = 31
LRU_WIDTH = D_MODEL
LRU_HEADS = 16
LRU_BLOCK = LRU_WIDTH // LRU_HEADS
LRU_CONV_WIDTH = 4
LRU_C = 8.0
D_FF = 3 * D_MODEL
N_MOD = 6
EPS = 1e-6
N_POOL_LAYERS = len(range(MIXER_POOL, DEPTH, N_MIXERS))
N_CONV_LAYERS = len(range(MIXER_CONV, DEPTH, N_MIXERS))
N_RGLRU_LAYERS = len(range(MIXER_RGLRU, DEPTH, N_MIXERS))

kernel_name = 'hybrid_pool_conformer_rglru_prefix_dit'


def _rmsnorm(x, g):
    xf = x.astype(jnp.float32)
    y = xf * lax.rsqrt(jnp.mean(xf * xf, axis=-1, keepdims=True) + EPS)
    return (y * g.astype(jnp.float32)).astype(x.dtype)


def _layernorm(x, g, b):
    xf = x.astype(jnp.float32)
    xc = xf - jnp.mean(xf, axis=-1, keepdims=True)
    var = jnp.mean(xc * xc, axis=-1, keepdims=True)
    return (xc * lax.rsqrt(var + EPS) * g.astype(jnp.float32) + b.astype(jnp.float32)).astype(x.dtype)


def _modulation(s, w, b):
    return jnp.split(s @ w + b, N_MOD, axis=-1)


def _dwconv1d(x, w, b, pad_lo, pad_hi):
    y = lax.conv_general_dilated(x, w[:, None, :].astype(x.dtype), window_strides=(1,),
                                 padding=[(pad_lo, pad_hi)], dimension_numbers=('NWC', 'WIO', 'NWC'),
                                 feature_group_count=x.shape[-1])
    return y + b


def _dwconv2d_grid(x, w, b, rows):
    B, L, C = x.shape
    xg = x.reshape(B, rows, GRID_W, C)
    y = lax.conv_general_dilated(xg, w[:, :, None, :].astype(x.dtype), window_strides=(1, 1),
                                 padding=[(1, 1), (1, 1)], dimension_numbers=('NHWC', 'HWIO', 'NHWC'),
                                 feature_group_count=C)
    return y.reshape(B, L, C) + b


def _pool_mixer(h, w, scale):
    B, L, D = h.shape
    hf = h.astype(jnp.float32)
    csum = jnp.concatenate([jnp.zeros((B, 1, D), jnp.float32), jnp.cumsum(hf, axis=1)], axis=1)
    t = jnp.arange(L)
    parts = []
    for g, win in enumerate(POOL_WINDOWS):
        lo = jnp.clip(t - win // 2, 0, L)
        hi = jnp.clip(t + win // 2, 0, L)
        sl = slice(g * POOL_GROUP, (g + 1) * POOL_GROUP)
        cs = csum[:, :, sl]
        mean = (cs[:, hi] - cs[:, lo]) / (hi - lo).astype(jnp.float32)[None, :, None]
        parts.append(mean - hf[:, :, sl])
    d = jnp.stack(parts, axis=2).astype(h.dtype)
    y = jnp.einsum('blgc,gcd->blgd', d, w).reshape(B, L, D)
    return y * scale


def _conformer_conv(h, w1, b1, dw, dw_b, ln_g, ln_b, w2, b2):
    a, g = jnp.split(h @ w1 + b1, 2, axis=-1)
    v = a * jax.nn.sigmoid(g)
    v = _dwconv1d(v, dw, dw_b, CONV_WIDTH // 2, CONV_WIDTH // 2)
    v = jax.nn.silu(_layernorm(v, ln_g, ln_b))
    return v @ w2 + b2


def _linear_scan(a, b, h0, reverse, keep_seq):
    def step(h, ab):
        h = ab[0] * h + ab[1]
        return h, (h if keep_seq else None)
    h_last, hs = lax.scan(step, h0, (jnp.swapaxes(a, 0, 1), jnp.swapaxes(b, 0, 1)), reverse=reverse)
    return (jnp.swapaxes(hs, 0, 1) if keep_seq else None), h_last


def _rglru_block(h, h0_fwd, h0_bwd, want_out, w_x, w_y, conv_w, conv_b, wa, ba, wi, bi, lam, w_out):
    B, L, _ = h.shape
    xc = _dwconv1d(h @ w_x, conv_w, conv_b, LRU_CONV_WIDTH // 2, LRU_CONV_WIDTH - 1 - LRU_CONV_WIDTH // 2)
    xh = xc.reshape(B, L, LRU_HEADS, LRU_BLOCK)

    def gate(w, bias):
        z = jnp.einsum('blhi,dhij->dblhj', xh, w).reshape(2, B, L, LRU_WIDTH)
        return jax.nn.sigmoid((z + bias[:, None, None, :]).astype(jnp.float32))

    r = gate(wa, ba)
    i_gate = gate(wi, bi)
    log_a = -LRU_C * r * jax.nn.softplus(-lam.astype(jnp.float32))[:, None, None, :]
    a = jnp.exp(log_a)
    u = jnp.sqrt(-jnp.expm1(2.0 * log_a)) * i_gate * xc.astype(jnp.float32)[None]
    s_f, hT_f = _linear_scan(a[0], u[0], h0_fwd, False, want_out)
    s_b, hT_b = _linear_scan(a[1], u[1], h0_bwd, True, want_out)
    if not want_out:
        return None, hT_f, hT_b
    y = (s_f + s_b).astype(h.dtype) * jax.nn.gelu(h @ w_y)
    return y @ w_out, hT_f, hT_b


def _conv_ffn(h, w_up, conv_w, conv_b, w_down, rows):
    u = h @ w_up
    if rows is None:
        u = _dwconv1d(u, conv_w[1], conv_b, 1, 1)
    else:
        u = _dwconv2d_grid(u, conv_w, conv_b, rows)
    g, v = jnp.split(u, 2, axis=-1)
    return (jax.nn.gelu(g) * v) @ w_down


def _lru_lambda(k):
    u = jax.random.uniform(k, (N_RGLRU_LAYERS, 2, LRU_WIDTH), jnp.float32, 0.9, 0.999)
    a0 = u ** (1.0 / LRU_C)
    return jnp.log(a0) - jnp.log1p(-a0)


def setup_inputs(seed: int = 0) -> dict:
    key = jax.random.key(seed)
    keys = jax.random.split(key, 40)
    ctr = [0]

    def nk():
        k = keys[ctr[0]]
        ctr[0] += 1
        return k

    def nrm(shape, scale):
        return jax.random.normal(nk(), shape, jnp.float32) * scale

    D, F, W = D_MODEL, D_FF, LRU_WIDTH
    nP, nC, nR = N_POOL_LAYERS, N_CONV_LAYERS, N_RGLRU_LAYERS
    return {
        'x': nrm((BATCH, SEQ, D), 1.0),
        'c': nrm((BATCH, D), 1.0),
        'ctx': nrm((BATCH, CTX_LEN, D), 1.0),
        'c_ctx': nrm((D,), 1.0),
        'norm1_g': 1.0 + nrm((DEPTH, D), 0.02),
        'norm2_g': 1.0 + nrm((DEPTH, D), 0.02),
        'mod_w': nrm((DEPTH, D, N_MOD * D), 0.5 * D ** -0.5),
        'mod_b': nrm((DEPTH, N_MOD * D), 0.02),
        'pool_w': nrm((nP, len(POOL_WINDOWS), POOL_GROUP, POOL_GROUP), POOL_GROUP ** -0.5),
        'pool_scale': 1.0 + nrm((nP, D), 0.1),
        'cv_w1': nrm((nC, D, 2 * D), D ** -0.5),
        'cv_b1': nrm((nC, 2 * D), 0.02),
        'cv_dw': nrm((nC, CONV_WIDTH, D), CONV_WIDTH ** -0.5),
        'cv_dw_b': nrm((nC, D), 0.02),
        'cv_ln_g': 1.0 + nrm((nC, D), 0.02),
        'cv_ln_b': nrm((nC, D), 0.02),
        'cv_w2': nrm((nC, D, D), D ** -0.5),
        'cv_b2': nrm((nC, D), 0.02),
        'lru_w_x': nrm((nR, D, W), D ** -0.5),
        'lru_w_y': nrm((nR, D, W), D ** -0.5),
        'lru_conv_w': nrm((nR, LRU_CONV_WIDTH, W), LRU_CONV_WIDTH ** -0.5),
        'lru_conv_b': nrm((nR, W), 0.02),
        'lru_wa': nrm((nR, 2, LRU_HEADS, LRU_BLOCK, LRU_BLOCK), LRU_BLOCK ** -0.5),
        'lru_ba': nrm((nR, 2, W), 0.02),
        'lru_wi': nrm((nR, 2, LRU_HEADS, LRU_BLOCK, LRU_BLOCK), LRU_BLOCK ** -0.5),
        'lru_bi': nrm((nR, 2, W), 0.02),
        'lru_lambda': _lru_lambda(nk()),
        'lru_w_out': nrm((nR, W, D), W ** -0.5),
        'ffn_w_up': nrm((DEPTH, D, 2 * F), D ** -0.5),
        'ffn_conv_w': nrm((DEPTH, 3, 3, 2 * F), 1.0 / 3.0),
        'ffn_conv_b': nrm((DEPTH, 2 * F), 0.02),
        'ffn_w_down': nrm((DEPTH, F, D), F ** -0.5),
        'final_g': 1.0 + nrm((D,), 0.02),
    }


def reference(x, c, ctx, c_ctx, norm1_g, norm2_g, mod_w, mod_b, pool_w, pool_scale,
              cv_w1, cv_b1, cv_dw, cv_dw_b, cv_ln_g, cv_ln_b, cv_w2, cv_b2,
              lru_w_x, lru_w_y, lru_conv_w, lru_conv_b, lru_wa, lru_ba, lru_wi, lru_bi, lru_lambda, lru_w_out,
              ffn_w_up, ffn_conv_w, ffn_conv_b, ffn_w_down, final_g):
    rows = x.shape[1] // GRID_W
    s_lat = jax.nn.silu(c)[:, None, :]
    s_ctx = jax.nn.silu(c_ctx)[None, None, :]
    for i in range(DEPTH):
        kind = i % N_MIXERS
        j = i // N_MIXERS
        ctx_later = any(k % N_MIXERS == MIXER_RGLRU for k in range(i + 1, DEPTH))
        ctx_mix = ctx_later or kind == MIXER_RGLRU
        sh1, sc1, g1, sh2, sc2, g2 = _modulation(s_lat, mod_w[i], mod_b[i])
        h_lat = _rmsnorm(x, norm1_g[i]) * (1 + sc1) + sh1
        if ctx_mix:
            csh1, csc1, cg1, csh2, csc2, cg2 = _modulation(s_ctx, mod_w[i], mod_b[i])
            h_ctx = _rmsnorm(ctx, norm1_g[i]) * (1 + csc1) + csh1
        y_ctx = None
        if kind == MIXER_POOL:
            y_lat = _pool_mixer(h_lat, pool_w[j], pool_scale[j])
            if ctx_later:
                y_ctx = _pool_mixer(h_ctx, pool_w[j], pool_scale[j])
        elif kind == MIXER_CONV:
            cvp = (cv_w1[j], cv_b1[j], cv_dw[j], cv_dw_b[j], cv_ln_g[j], cv_ln_b[j], cv_w2[j], cv_b2[j])
            y_lat = _conformer_conv(h_lat, *cvp)
            if ctx_later:
                y_ctx = _conformer_conv(h_ctx, *cvp)
        else:
            lp = (lru_w_x[j], lru_w_y[j], lru_conv_w[j], lru_conv_b[j], lru_wa[j], lru_ba[j],
                  lru_wi[j], lru_bi[j], lru_lambda[j], lru_w_out[j])
            zeros = jnp.zeros((ctx.shape[0], LRU_WIDTH), jnp.float32)
            y_ctx, hT_f, hT_b = _rglru_block(h_ctx, zeros, zeros, ctx_later, *lp)
            y_lat, _, _ = _rglru_block(h_lat, hT_f, hT_b, True, *lp)
        x = x + g1 * y_lat
        h2 = _rmsnorm(x, norm2_g[i]) * (1 + sc2) + sh2
        x = x + g2 * _conv_ffn(h2, ffn_w_up[i], ffn_conv_w[i], ffn_conv_b[i], ffn_w_down[i], rows)
        if ctx_later:
            ctx = ctx + cg1 * y_ctx
            hc2 = _rmsnorm(ctx, norm2_g[i]) * (1 + csc2) + csh2
            ctx = ctx + cg2 * _conv_ffn(hc2, ffn_w_up[i], ffn_conv_w[i], ffn_conv_b[i], ffn_w_down[i], None)
    return _rmsnorm(x, final_g)
```

```python
import functools

import jax
import jax.numpy as jnp
from jax import lax
from jax.experimental import pallas as pl
from jax.experimental.pallas import tpu as pltpu

GRID_W = 64
N_MIXERS = 3
POOL_WINDOWS = (2, 4, 8, 16)
CONV_WIDTH = 31
LRU_HEADS = 16
LRU_CONV_WIDTH = 4
LRU_C = 8.0
N_MOD = 6
EPS = 1e-6

LANES = 128
SUBLANES = 8
VMEM_LIMIT_BYTES = 56 * 1024 * 1024
SEQ_TILE = 512
FFN_ROWS = 8
FFN_CHUNK = 512
CONV_HALO = 16
CONV_ROWS = 64
CONV_TILE = 256
PROJ_COLS = 512
MOD_COLS = 1024

bf16 = jnp.bfloat16
f32 = jnp.float32


def _params(*sem):
    return pltpu.CompilerParams(dimension_semantics=sem, vmem_limit_bytes=VMEM_LIMIT_BYTES)


def _tile(L, pref):
    t = min(L, pref)
    assert L % t == 0 and t % SUBLANES == 0
    return t


def _rms_mod(x, g, sc, sh):
    y = x * lax.rsqrt(jnp.mean(x * x, axis=-1, keepdims=True) + EPS)
    return (y * g) * (1.0 + sc) + sh


def _row_spec(n):
    return pl.BlockSpec((1, n), lambda *_: (0, 0))


def _seg_spec(d, seg_of):
    return pl.BlockSpec((1, 1, d), lambda i, *_: (seg_of(i), 0, 0))


def _halo_specs(T, hb, n_rows, d):
    r = T // hb
    last = n_rows // hb - 1
    prev = pl.BlockSpec((hb, d), lambda i, *_: (jnp.maximum(i * r - 1, 0), 0))
    nxt = pl.BlockSpec((hb, d), lambda i, *_: (jnp.minimum((i + 1) * r, last), 0))
    return prev, nxt


def _mod_kernel(c_ref, w_ref, b_ref, o_ref):
    c = c_ref[...]
    s = (c * jax.nn.sigmoid(c)).astype(bf16)
    o_ref[...] = jnp.dot(s, w_ref[...].astype(bf16), preferred_element_type=f32) + b_ref[...]


def _modulation(c8, mod_w, mod_b):
    depth, d, n = mod_w.shape
    tn = min(MOD_COLS, n)
    return pl.pallas_call(
        _mod_kernel,
        grid=(depth, n // tn),
        in_specs=[pl.BlockSpec((SUBLANES, d), lambda l, j: (0, 0)),
                  pl.BlockSpec((None, d, tn), lambda l, j: (l, 0, j)),
                  pl.BlockSpec((None, 1, tn), lambda l, j: (l, 0, j))],
        out_specs=pl.BlockSpec((None, SUBLANES, tn), lambda l, j: (l, 0, j)),
        out_shape=jax.ShapeDtypeStruct((depth, SUBLANES, n), f32),
        compiler_params=_params("parallel", "parallel"),
        name="modulation",
    )(c8, mod_w, mod_b.reshape(depth, 1, n))


def _prenorm_dual_kernel(*refs, mode, has_bias):
    x_ref, g_ref, sc_ref, sh_ref, wa_ref, wb_ref = refs[:6]
    refs = refs[6:]
    if has_bias:
        ba_ref, bb_ref = refs[:2]
        refs = refs[2:]
    h_ref = refs[-1]
    outs = refs[:-1]

    @pl.when(pl.program_id(1) == 0)
    def _():
        h_ref[...] = _rms_mod(x_ref[...], g_ref[...], sc_ref[0], sh_ref[0]).astype(bf16)

    h = h_ref[...]
    a = jnp.dot(h, wa_ref[...], preferred_element_type=f32)
    b = jnp.dot(h, wb_ref[...], preferred_element_type=f32)
    if has_bias:
        a = a + ba_ref[...]
        b = b + bb_ref[...]
    if mode == "glu":
        outs[0][...] = a * jax.nn.sigmoid(b)
    else:
        outs[0][...] = a
        outs[1][...] = jax.nn.gelu(b)


def _prenorm_dual(x, g, sc, sh, wa, wb, ba, bb, *, mode, L, wb_off, name):
    n_rows, d = x.shape
    T = _tile(L, SEQ_TILE)
    tps = L // T
    nseg = sc.shape[0]
    seg_of = (lambda i: i // tps) if nseg > 1 else (lambda i: 0)
    n_out = d
    tn = min(PROJ_COLS, n_out)
    has_bias = ba is not None
    in_specs = [pl.BlockSpec((T, d), lambda i, j: (i, 0)), _row_spec(d), _seg_spec(d, seg_of), _seg_spec(d, seg_of),
                pl.BlockSpec((d, tn), lambda i, j: (0, j)),
                pl.BlockSpec((d, tn), lambda i, j: (0, j + wb_off))]
    args = [x, g, sc, sh, wa, wb]
    if has_bias:
        in_specs += [pl.BlockSpec((1, tn), lambda i, j: (0, j)), pl.BlockSpec((1, tn), lambda i, j: (0, j + wb_off))]
        args += [ba, bb]
    o_spec = pl.BlockSpec((T, tn), lambda i, j: (i, j))
    o_shape = jax.ShapeDtypeStruct((n_rows, n_out), f32)
    n_o = 1 if mode == "glu" else 2
    res = pl.pallas_call(
        functools.partial(_prenorm_dual_kernel, mode=mode, has_bias=has_bias),
        grid=(n_rows // T, n_out // tn),
        in_specs=in_specs,
        out_specs=[o_spec] * n_o,
        out_shape=[o_shape] * n_o,
        scratch_shapes=[pltpu.VMEM((T, d), bf16)],
        compiler_params=_params("parallel", "arbitrary"),
        name=name,
    )(*args)
    return res[0] if n_o == 1 else res


def _pool_kernel(x_ref, xp_ref, xn_ref, g_ref, sc_ref, sh_ref, gate_ref, w_ref, scale_ref, o_ref, hbuf,
                 *, T, L, tps, G):
    it = pl.program_id(0) % tps
    g = g_ref[...]
    sc = sc_ref[0]
    sh = sh_ref[0]
    gate = gate_ref[0]
    hb = SUBLANES
    x = x_ref[...]
    h = _rms_mod(x, g, sc, sh)
    hbuf[0:hb] = jnp.where(it == 0, 0.0, _rms_mod(xp_ref[...], g, sc, sh))
    hbuf[hb:hb + T] = h
    hbuf[hb + T:2 * hb + T] = jnp.where(it == tps - 1, 0.0, _rms_mod(xn_ref[...], g, sc, sh))
    tpos = it * T + lax.broadcasted_iota(jnp.int32, (T, 1), 0)
    for gi, win in enumerate(POOL_WINDOWS):
        half = win // 2
        c0 = gi * G
        s = hbuf[hb - half:hb - half + T, c0:c0 + G]
        for o in range(-half + 1, half):
            s = s + hbuf[hb + o:hb + o + T, c0:c0 + G]
        cnt = (jnp.minimum(tpos + half, L) - jnp.maximum(tpos - half, 0)).astype(f32)
        dlt = (s / cnt - h[:, c0:c0 + G]).astype(bf16)
        y = jnp.dot(dlt, w_ref[gi], preferred_element_type=f32) * scale_ref[:, c0:c0 + G]
        o_ref[:, c0:c0 + G] = x[:, c0:c0 + G] + gate[:, c0:c0 + G] * y


def _pool_layer(x, g, sc, sh, gate, w, scale, *, L, name):
    n_rows, d = x.shape
    T = _tile(L, SEQ_TILE)
    tps = L // T
    nseg = sc.shape[0]
    seg_of = (lambda i: i // tps) if nseg > 1 else (lambda i: 0)
    ng, G, _ = w.shape
    assert max(POOL_WINDOWS) // 2 <= SUBLANES
    prev, nxt = _halo_specs(T, SUBLANES, n_rows, d)
    return pl.pallas_call(
        functools.partial(_pool_kernel, T=T, L=L, tps=tps, G=G),
        grid=(n_rows // T,),
        in_specs=[pl.BlockSpec((T, d), lambda i: (i, 0)), prev, nxt, _row_spec(d),
                  _seg_spec(d, seg_of), _seg_spec(d, seg_of), _seg_spec(d, seg_of),
                  pl.BlockSpec((ng, G, G), lambda i: (0, 0, 0)), _row_spec(d)],
        out_specs=pl.BlockSpec((T, d), lambda i: (i, 0)),
        out_shape=jax.ShapeDtypeStruct((n_rows, d), f32),
        scratch_shapes=[pltpu.VMEM((T + 2 * SUBLANES, d), f32)],
        compiler_params=_params("parallel"),
        name=name,
    )(x, x, x, g, sc, sh, gate, w, scale)


def _cconv_kernel(v_ref, vp_ref, vn_ref, dw_ref, dwb_ref, lng_ref, lnb_ref, w2_ref, b2_ref, x_ref, gate_ref,
                  o_ref, vbuf, cbuf, *, T, tps):
    it = pl.program_id(0) % tps
    hb = CONV_HALO
    d = v_ref.shape[1]
    vbuf[0:hb] = jnp.where(it == 0, 0.0, vp_ref[...])
    vbuf[hb:hb + T] = v_ref[...]
    vbuf[hb + T:2 * hb + T] = jnp.where(it == tps - 1, 0.0, vn_ref[...])
    off = hb - CONV_WIDTH // 2
    rb = min(CONV_ROWS, T)

    def col_body(c, carry):
        c0 = pl.multiple_of(c * LANES, LANES)
        w = dw_ref[:, pl.ds(c0, LANES)]
        for r0 in range(0, T, rb):
            acc = jnp.zeros((rb, LANES), f32)
            for k in range(CONV_WIDTH):
                acc = acc + w[k:k + 1, :] * vbuf[pl.ds(r0 + k + off, rb), pl.ds(c0, LANES)]
            cbuf[pl.ds(r0, rb), pl.ds(c0, LANES)] = acc
        return carry

    lax.fori_loop(0, d // LANES, col_body, 0)
    cv = cbuf[...] + dwb_ref[...]
    xc = cv - jnp.mean(cv, axis=-1, keepdims=True)
    var = jnp.mean(xc * xc, axis=-1, keepdims=True)
    ln = xc * lax.rsqrt(var + EPS) * lng_ref[...] + lnb_ref[...]
    act = (ln * jax.nn.sigmoid(ln)).astype(bf16)
    y = jnp.dot(act, w2_ref[...], preferred_element_type=f32) + b2_ref[...]
    o_ref[...] = x_ref[...] + gate_ref[0] * y


def _cconv_layer(v, dw, dwb, lng, lnb, w2, b2, x, gate, *, L, name):
    n_rows, d = x.shape
    T = _tile(L, CONV_TILE)
    tps = L // T
    nseg = gate.shape[0]
    seg_of = (lambda i: i // tps) if nseg > 1 else (lambda i: 0)
    prev, nxt = _halo_specs(T, CONV_HALO, n_rows, d)
    tile = pl.BlockSpec((T, d), lambda i: (i, 0))
    return pl.pallas_call(
        functools.partial(_cconv_kernel, T=T, tps=tps),
        grid=(n_rows // T,),
        in_specs=[tile, prev, nxt, pl.BlockSpec((CONV_WIDTH, d), lambda i: (0, 0)), _row_spec(d), _row_spec(d),
                  _row_spec(d), pl.BlockSpec((d, d), lambda i: (0, 0)), _row_spec(d), tile, _seg_spec(d, seg_of)],
        out_specs=tile,
        out_shape=jax.ShapeDtypeStruct((n_rows, d), f32),
        scratch_shapes=[pltpu.VMEM((T + 2 * CONV_HALO, d), f32), pltpu.VMEM((T, d), f32)],
        compiler_params=_params("parallel"),
        name=name,
    )(v, v, v, dw, dwb, lng, lnb, w2, b2, x, gate)


def _lru_scan_kernel(xw_ref, xp_ref, xn_ref, cw_ref, cb_ref, wg_ref, ba_ref, bi_ref, lam_ref, h0_ref,
                     s_ref, ht_ref, xbuf, abuf, ubuf, st_ref, *, T, tps, reverse):
    j = pl.program_id(1)
    it = (tps - 1 - j) if reverse else j
    hb = SUBLANES
    d = xw_ref.shape[1]
    blk = d // LRU_HEADS

    @pl.when(j == 0)
    def _():
        st_ref[...] = h0_ref[0]

    xbuf[0:hb] = jnp.where(it == 0, 0.0, xp_ref[...])
    xbuf[hb:hb + T] = xw_ref[...]
    xbuf[hb + T:2 * hb + T] = jnp.where(it == tps - 1, 0.0, xn_ref[...])
    pad_lo = LRU_CONV_WIDTH // 2
    softplus_neg_lam = jax.nn.softplus(-lam_ref[...])
    for hd in range(LRU_HEADS):
        c0 = hd * blk
        xc = cb_ref[:, c0:c0 + blk] + jnp.zeros((T, blk), f32)
        for k in range(LRU_CONV_WIDTH):
            xc = xc + cw_ref[k:k + 1, c0:c0 + blk] * xbuf[hb + k - pad_lo:hb + k - pad_lo + T, c0:c0 + blk]
        z = jnp.dot(xc.astype(bf16), wg_ref[hd], preferred_element_type=f32)
        r = jax.nn.sigmoid(z[:, :blk] + ba_ref[:, c0:c0 + blk])
        ig = jax.nn.sigmoid(z[:, blk:] + bi_ref[:, c0:c0 + blk])
        log_a = -LRU_C * r * softplus_neg_lam[:, c0:c0 + blk]
        a = jnp.exp(log_a)
        one_minus_a2 = -jnp.tanh(log_a) * (a * a + 1.0)
        abuf[:, c0:c0 + blk] = a
        ubuf[:, c0:c0 + blk] = jnp.sqrt(one_minus_a2) * ig * xc

    n_grp = T // SUBLANES

    def grp(gi, h):
        g = (n_grp - 1 - gi) if reverse else gi
        r0 = pl.multiple_of(g * SUBLANES, SUBLANES)
        a = abuf[pl.ds(r0, SUBLANES), :]
        u = ubuf[pl.ds(r0, SUBLANES), :]
        rows = [None] * SUBLANES
        for r in (range(SUBLANES - 1, -1, -1) if reverse else range(SUBLANES)):
            h = a[r:r + 1, :] * h + u[r:r + 1, :]
            rows[r] = h
        s_ref[pl.ds(r0, SUBLANES), :] = jnp.concatenate(rows, axis=0)
        return h

    h_fin = lax.fori_loop(0, n_grp, grp, st_ref[...])
    st_ref[...] = h_fin
    ht_ref[0] = h_fin


def _lru_scan(xw, cw, cb, wg, ba, bi, lam, h0, *, L, reverse, name):
    n_rows, d = xw.shape
    nb = n_rows // L
    T = _tile(L, SEQ_TILE)
    tps = L // T

    def tile_of(b, j):
        return b * tps + ((tps - 1 - j) if reverse else j)

    r = T // SUBLANES
    last = n_rows // SUBLANES - 1
    tile = pl.BlockSpec((T, d), lambda b, j: (tile_of(b, j), 0))
    prev = pl.BlockSpec((SUBLANES, d), lambda b, j: (jnp.maximum(tile_of(b, j) * r - 1, 0), 0))
    nxt = pl.BlockSpec((SUBLANES, d), lambda b, j: (jnp.minimum((tile_of(b, j) + 1) * r, last), 0))
    row = pl.BlockSpec((1, d), lambda b, j: (0, 0))
    state = pl.BlockSpec((1, 1, d), lambda b, j: (b, 0, 0))
    return pl.pallas_call(
        functools.partial(_lru_scan_kernel, T=T, tps=tps, reverse=reverse),
        grid=(nb, tps),
        in_specs=[tile, prev, nxt, pl.BlockSpec((LRU_CONV_WIDTH, d), lambda b, j: (0, 0)), row,
                  pl.BlockSpec(wg.shape, lambda b, j: (0, 0, 0)), row, row, row, state],
        out_specs=[tile, state],
        out_shape=[jax.ShapeDtypeStruct((n_rows, d), f32), jax.ShapeDtypeStruct((nb, 1, d), f32)],
        scratch_shapes=[pltpu.VMEM((T + 2 * SUBLANES, d), f32), pltpu.VMEM((T, d), f32), pltpu.VMEM((T, d), f32),
                        pltpu.VMEM((1, d), f32)],
        compiler_params=_params("parallel", "arbitrary"),
        name=name,
    )(xw, xw, xw, cw, cb, wg, ba, bi, lam, h0)


def _lru_out_kernel(sf_ref, sb_ref, gy_ref, w_ref, x_ref, gate_ref, o_ref):
    y = ((sf_ref[...] + sb_ref[...]) * gy_ref[...]).astype(bf16)
    o_ref[...] = x_ref[...] + gate_ref[0] * jnp.dot(y, w_ref[...], preferred_element_type=f32)


def _lru_out(sf, sb, gy, w_out, x, gate, *, L, name):
    n_rows, d = x.shape
    T = _tile(L, CONV_TILE)
    tps = L // T
    nseg = gate.shape[0]
    seg_of = (lambda i: i // tps) if nseg > 1 else (lambda i: 0)
    tile = pl.BlockSpec((T, d), lambda i: (i, 0))
    return pl.pallas_call(
        _lru_out_kernel,
        grid=(n_rows // T,),
        in_specs=[tile, tile, tile, pl.BlockSpec((d, d), lambda i: (0, 0)), tile, _seg_spec(d, seg_of)],
        out_specs=tile,
        out_shape=jax.ShapeDtypeStruct((n_rows, d), f32),
        compiler_params=_params("parallel"),
        name=name,
    )(sf, sb, gy, w_out, x, gate)


def _ffn_kernel(*refs, T, W, halo, tps, fc, final):
    x_ref = refs[0]
    refs = refs[1:]
    if halo:
        xt_ref, xb_ref = refs[:2]
        refs = refs[2:]
    g_ref, sc_ref, sh_ref, gate_ref, wg_ref, wv_ref, cwg_ref, cwv_ref, cbg_ref, cbv_ref, wd_ref = refs[:11]
    refs = refs[11:]
    if final:
        fg_ref = refs[0]
        refs = refs[1:]
    o_ref, h_ref, acc_ref = refs
    i = pl.program_id(0)
    k = pl.program_id(1)
    it = i % tps
    n_rows = T + 2 * halo

    @pl.when(k == 0)
    def _():
        g = g_ref[...]
        sc = sc_ref[0]
        sh = sh_ref[0]
        h_ref[halo:halo + T] = _rms_mod(x_ref[...], g, sc, sh).astype(bf16)
        if halo:
            top = _rms_mod(xt_ref[...], g, sc, sh)
            bot = _rms_mod(xb_ref[...], g, sc, sh)
            h_ref[0:halo] = jnp.where(it == 0, 0.0, top).astype(bf16)
            h_ref[halo + T:n_rows] = jnp.where(it == tps - 1, 0.0, bot).astype(bf16)
        acc_ref[...] = jnp.zeros_like(acc_ref)

    h = h_ref[...]
    wpos = lax.broadcasted_iota(jnp.int32, (n_rows, 1), 0) % W

    def conv(u, cw_ref, cb_ref):
        u_l = jnp.where(wpos == 0, 0.0, pltpu.roll(u, 1, axis=0))
        u_r = jnp.where(wpos == W - 1, 0.0, pltpu.roll(u, n_rows - 1, axis=0))
        out = cb_ref[...] + jnp.zeros((T, fc), f32)
        for dr in ((0, 1, 2) if halo else (1,)):
            o = halo + (dr - 1) * W
            out = out + cw_ref[3 * dr:3 * dr + 1, :] * u_l[o:o + T]
            out = out + cw_ref[3 * dr + 1:3 * dr + 2, :] * u[o:o + T]
            out = out + cw_ref[3 * dr + 2:3 * dr + 3, :] * u_r[o:o + T]
        return out

    ug = conv(jnp.dot(h, wg_ref[...], preferred_element_type=f32), cwg_ref, cbg_ref)
    uv = conv(jnp.dot(h, wv_ref[...], preferred_element_type=f32), cwv_ref, cbv_ref)
    act = (jax.nn.gelu(ug) * uv).astype(bf16)
    acc_ref[...] += jnp.dot(act, wd_ref[...], preferred_element_type=f32)

    @pl.when(k == pl.num_programs(1) - 1)
    def _():
        y = x_ref[...] + gate_ref[0] * acc_ref[...]
        if final:
            y = y * lax.rsqrt(jnp.mean(y * y, axis=-1, keepdims=True) + EPS) * fg_ref[...]
        o_ref[...] = y


def _ffn_layer(x, g, sc, sh, gate, w_up, cw, cb, w_down, final_g, *, L, two_d, name):
    n_rows, d = x.shape
    F = w_down.shape[0]
    fc = min(FFN_CHUNK, F)
    if two_d:
        W = GRID_W
        T = min(L, FFN_ROWS * W)
        halo = W
    else:
        W = T = L
        halo = 0
    assert L % T == 0 and F % fc == 0
    tps = L // T
    nk = F // fc
    nseg = sc.shape[0]
    seg_of = (lambda i: i // tps) if nseg > 1 else (lambda i: 0)
    tile = pl.BlockSpec((T, d), lambda i, k: (i, 0))
    in_specs = [tile]
    args = [x]
    if halo:
        in_specs += list(_halo_specs(T, halo, n_rows, d))
        args += [x, x]
    in_specs += [_row_spec(d), _seg_spec(d, seg_of), _seg_spec(d, seg_of), _seg_spec(d, seg_of),
                 pl.BlockSpec((d, fc), lambda i, k: (0, k)), pl.BlockSpec((d, fc), lambda i, k: (0, k + nk)),
                 pl.BlockSpec((9, fc), lambda i, k: (0, k)), pl.BlockSpec((9, fc), lambda i, k: (0, k + nk)),
                 pl.BlockSpec((1, fc), lambda i, k: (0, k)), pl.BlockSpec((1, fc), lambda i, k: (0, k + nk)),
                 pl.BlockSpec((fc, d), lambda i, k: (k, 0))]
    args += [g, sc, sh, gate, w_up, w_up, cw, cw, cb, cb, w_down]
    final = final_g is not None
    if final:
        in_specs.append(_row_spec(d))
        args.append(final_g)
    return pl.pallas_call(
        functools.partial(_ffn_kernel, T=T, W=W, halo=halo, tps=tps, fc=fc, final=final),
        grid=(n_rows // T, nk),
        in_specs=in_specs,
        out_specs=tile,
        out_shape=jax.ShapeDtypeStruct((n_rows, d), f32),
        scratch_shapes=[pltpu.VMEM((T + 2 * halo, d), bf16), pltpu.VMEM((T, d), f32)],
        compiler_params=_params("parallel", "arbitrary"),
        name=name,
    )(*args)


def kernel(x, c, ctx, c_ctx, norm1_g, norm2_g, mod_w, mod_b, pool_w, pool_scale, cv_w1, cv_b1, cv_dw, cv_dw_b, cv_ln_g, cv_ln_b, cv_w2, cv_b2, lru_w_x, lru_w_y, lru_conv_w, lru_conv_b, lru_wa, lru_ba, lru_wi, lru_bi, lru_lambda, lru_w_out, ffn_w_up, ffn_conv_w, ffn_conv_b, ffn_w_down, final_g):
    B, L, D = x.shape
    Lc = ctx.shape[1]
    depth = norm1_g.shape[0]
    F2 = ffn_w_up.shape[2]

    c8 = jnp.concatenate([c, c_ctx[None, :], jnp.zeros((SUBLANES - B - 1, D), f32)], axis=0)
    mod = _modulation(c8, mod_w, mod_b).reshape(depth, SUBLANES, N_MOD, 1, D)

    xl = x.reshape(B * L, D)
    xc = ctx.reshape(B * Lc, D)
    row = lambda v: v.reshape(1, -1)

    for i in range(depth):
        kind = i % N_MIXERS
        j = i // N_MIXERS
        ctx_later = any(k % N_MIXERS == 2 for k in range(i + 1, depth))
        ctx_mix = ctx_later or kind == 2
        m_lat = [mod[i, 0:B, q] for q in range(N_MOD)]
        m_ctx = [mod[i, B:B + 1, q] for q in range(N_MOD)]
        streams = [("lat", xl, L, m_lat, True)]
        if ctx_mix:
            streams.append(("ctx", xc, Lc, m_ctx, ctx_later))
        g1n = row(norm1_g[i])
        g2n = row(norm2_g[i])
        new = {}
        if kind == 0:
            pw = pool_w[j].astype(bf16)
            for nm, xs, Ls, m, full in streams:
                if full:
                    new[nm] = _pool_layer(xs, g1n, m[1], m[0], m[2], pw, row(pool_scale[j]), L=Ls,
                                          name=f"pool{i}_{nm}")
        elif kind == 1:
            w1 = cv_w1[j].astype(bf16)
            w2 = cv_w2[j].astype(bf16)
            b1 = row(cv_b1[j])
            for nm, xs, Ls, m, full in streams:
                if full:
                    v = _prenorm_dual(xs, g1n, m[1], m[0], w1, w1, b1, b1, mode="glu", L=Ls,
                                      wb_off=D // min(PROJ_COLS, D), name=f"cv_glu{i}_{nm}")
                    new[nm] = _cconv_layer(v, cv_dw[j], row(cv_dw_b[j]), row(cv_ln_g[j]), row(cv_ln_b[j]), w2,
                                           row(cv_b2[j]), xs, m[2], L=Ls, name=f"cv_tail{i}_{nm}")
        else:
            wx = lru_w_x[j].astype(bf16)
            wy = lru_w_y[j].astype(bf16)
            wo = lru_w_out[j].astype(bf16)
            wg = jnp.concatenate([lru_wa[j], lru_wi[j]], axis=-1).astype(bf16)
            h0 = [jnp.zeros((B, 1, D), f32)] * 2
            for nm, xs, Ls, m, full in reversed(streams):
                xw, gy = _prenorm_dual(xs, g1n, m[1], m[0], wx, wy, None, None, mode="lru", L=Ls,
                                       wb_off=0, name=f"lru_in{i}_{nm}")
                s, ht = [], []
                for dr in range(2):
                    sd, hd = _lru_scan(xw, lru_conv_w[j], row(lru_conv_b[j]), wg[dr], row(lru_ba[j, dr]),
                                       row(lru_bi[j, dr]), row(lru_lambda[j, dr]), h0[dr], L=Ls,
                                       reverse=bool(dr), name=f"lru_scan{i}_{nm}{dr}")
                    s.append(sd)
                    ht.append(hd)
                h0 = ht
                if full:
                    new[nm] = _lru_out(s[0], s[1], gy, wo, xs, m[2], L=Ls, name=f"lru_out{i}_{nm}")
        w_up = ffn_w_up[i].astype(bf16)
        w_dn = ffn_w_down[i].astype(bf16)
        cw = ffn_conv_w[i].reshape(9, F2)
        cb = row(ffn_conv_b[i])
        fin = row(final_g) if i == depth - 1 else None
        xl = _ffn_layer(new["lat"], g2n, m_lat[4], m_lat[3], m_lat[5], w_up, cw, cb, w_dn, fin, L=L, two_d=True,
                        name=f"ffn{i}_lat")
        if ctx_later:
            xc = _ffn_layer(new["ctx"], g2n, m_ctx[4], m_ctx[3], m_ctx[5], w_up, cw, cb, w_dn, None, L=Lc,
                            two_d=False, name=f"ffn{i}_ctx")
    return xl.reshape(B, L, D)
```

```python
import functools

import jax
import jax.numpy as jnp
from jax import lax
from jax.experimental import pallas as pl
from jax.experimental.pallas import tpu as pltpu

GRID_W = 64
N_MIXERS = 3
POOL_WINDOWS = (2, 4, 8, 16)
CONV_WIDTH = 31
LRU_HEADS = 16
LRU_CONV_WIDTH = 4
LRU_C = 8.0
N_MOD = 6
EPS = 1e-6

LANES = 128
SUBLANES = 8
VMEM_LIMIT_BYTES = 56 * 1024 * 1024
SEQ_TILE = 512
FFN_ROWS = 16
FFN_CHUNK = 512
FFN_SUB = 256
CONV_HALO = 16
CONV_ROWS = 64
CONV_TILE = 256
PROJ_COLS = 512
MOD_COLS = 1024

bf16 = jnp.bfloat16
f32 = jnp.float32


def _params(*sem):
    return pltpu.CompilerParams(dimension_semantics=sem, vmem_limit_bytes=VMEM_LIMIT_BYTES)


def _tile(L, pref):
    t = min(L, pref)
    assert L % t == 0 and t % SUBLANES == 0
    return t


def _rms_mod(x, g, sc, sh):
    y = x * lax.rsqrt(jnp.mean(x * x, axis=-1, keepdims=True) + EPS)
    return (y * g) * (1.0 + sc) + sh


def _row_spec(n):
    return pl.BlockSpec((1, n), lambda *_: (0, 0))


def _seg_spec(d, seg_of):
    return pl.BlockSpec((1, 1, d), lambda i, *_: (seg_of(i), 0, 0))


def _halo_specs(T, hb, n_rows, d):
    r = T // hb
    last = n_rows // hb - 1
    prev = pl.BlockSpec((hb, d), lambda i, *_: (jnp.maximum(i * r - 1, 0), 0))
    nxt = pl.BlockSpec((hb, d), lambda i, *_: (jnp.minimum((i + 1) * r, last), 0))
    return prev, nxt


def _mod_kernel(c_ref, w_ref, b_ref, o_ref):
    c = c_ref[...]
    s = (c * jax.nn.sigmoid(c)).astype(bf16)
    o_ref[...] = jnp.dot(s, w_ref[...].astype(bf16), preferred_element_type=f32) + b_ref[...]


def _modulation(c8, mod_w, mod_b):
    depth, d, n = mod_w.shape
    tn = min(MOD_COLS, n)
    return pl.pallas_call(
        _mod_kernel,
        grid=(depth, n // tn),
        in_specs=[pl.BlockSpec((SUBLANES, d), lambda l, j: (0, 0)),
                  pl.BlockSpec((None, d, tn), lambda l, j: (l, 0, j)),
                  pl.BlockSpec((None, 1, tn), lambda l, j: (l, 0, j))],
        out_specs=pl.BlockSpec((None, SUBLANES, tn), lambda l, j: (l, 0, j)),
        out_shape=jax.ShapeDtypeStruct((depth, SUBLANES, n), f32),
        compiler_params=_params("parallel", "parallel"),
        name="modulation",
    )(c8, mod_w, mod_b.reshape(depth, 1, n))


def _prenorm_dual_kernel(*refs, mode, has_bias):
    x_ref, g_ref, sc_ref, sh_ref, wa_ref, wb_ref = refs[:6]
    refs = refs[6:]
    if has_bias:
        ba_ref, bb_ref = refs[:2]
        refs = refs[2:]
    h_ref = refs[-1]
    outs = refs[:-1]

    @pl.when(pl.program_id(1) == 0)
    def _():
        h_ref[...] = _rms_mod(x_ref[...], g_ref[...], sc_ref[0], sh_ref[0]).astype(bf16)

    h = h_ref[...]
    a = jnp.dot(h, wa_ref[...], preferred_element_type=f32)
    b = jnp.dot(h, wb_ref[...], preferred_element_type=f32)
    if has_bias:
        a = a + ba_ref[...]
        b = b + bb_ref[...]
    if mode == "glu":
        outs[0][...] = a * jax.nn.sigmoid(b)
    else:
        outs[0][...] = a
        outs[1][...] = jax.nn.gelu(b)


def _prenorm_dual(x, g, sc, sh, wa, wb, ba, bb, *, mode, L, wb_off, name):
    n_rows, d = x.shape
    T = _tile(L, SEQ_TILE)
    tps = L // T
    nseg = sc.shape[0]
    seg_of = (lambda i: i // tps) if nseg > 1 else (lambda i: 0)
    n_out = d
    tn = min(PROJ_COLS, n_out)
    has_bias = ba is not None
    in_specs = [pl.BlockSpec((T, d), lambda i, j: (i, 0)), _row_spec(d), _seg_spec(d, seg_of), _seg_spec(d, seg_of),
                pl.BlockSpec((d, tn), lambda i, j: (0, j)),
                pl.BlockSpec((d, tn), lambda i, j: (0, j + wb_off))]
    args = [x, g, sc, sh, wa, wb]
    if has_bias:
        in_specs += [pl.BlockSpec((1, tn), lambda i, j: (0, j)), pl.BlockSpec((1, tn), lambda i, j: (0, j + wb_off))]
        args += [ba, bb]
    o_spec = pl.BlockSpec((T, tn), lambda i, j: (i, j))
    o_shape = jax.ShapeDtypeStruct((n_rows, n_out), f32)
    n_o = 1 if mode == "glu" else 2
    res = pl.pallas_call(
        functools.partial(_prenorm_dual_kernel, mode=mode, has_bias=has_bias),
        grid=(n_rows // T, n_out // tn),
        in_specs=in_specs,
        out_specs=[o_spec] * n_o,
        out_shape=[o_shape] * n_o,
        scratch_shapes=[pltpu.VMEM((T, d), bf16)],
        compiler_params=_params("parallel", "arbitrary"),
        name=name,
    )(*args)
    return res[0] if n_o == 1 else res


def _pool_kernel(x_ref, xp_ref, xn_ref, g_ref, sc_ref, sh_ref, gate_ref, w_ref, scale_ref, o_ref, hbuf,
                 *, T, L, tps, G):
    it = pl.program_id(0) % tps
    g = g_ref[...]
    sc = sc_ref[0]
    sh = sh_ref[0]
    gate = gate_ref[0]
    hb = SUBLANES
    x = x_ref[...]
    h = _rms_mod(x, g, sc, sh)
    hbuf[0:hb] = jnp.where(it == 0, 0.0, _rms_mod(xp_ref[...], g, sc, sh))
    hbuf[hb:hb + T] = h
    hbuf[hb + T:2 * hb + T] = jnp.where(it == tps - 1, 0.0, _rms_mod(xn_ref[...], g, sc, sh))
    tpos = it * T + lax.broadcasted_iota(jnp.int32, (T, 1), 0)
    for gi, win in enumerate(POOL_WINDOWS):
        half = win // 2
        c0 = gi * G
        s = hbuf[hb - half:hb - half + T, c0:c0 + G]
        for o in range(-half + 1, half):
            s = s + hbuf[hb + o:hb + o + T, c0:c0 + G]
        cnt = (jnp.minimum(tpos + half, L) - jnp.maximum(tpos - half, 0)).astype(f32)
        dlt = (s / cnt - h[:, c0:c0 + G]).astype(bf16)
        y = jnp.dot(dlt, w_ref[gi], preferred_element_type=f32) * scale_ref[:, c0:c0 + G]
        o_ref[:, c0:c0 + G] = x[:, c0:c0 + G] + gate[:, c0:c0 + G] * y


def _pool_layer(x, g, sc, sh, gate, w, scale, *, L, name):
    n_rows, d = x.shape
    T = _tile(L, SEQ_TILE)
    tps = L // T
    nseg = sc.shape[0]
    seg_of = (lambda i: i // tps) if nseg > 1 else (lambda i: 0)
    ng, G, _ = w.shape
    assert max(POOL_WINDOWS) // 2 <= SUBLANES
    prev, nxt = _halo_specs(T, SUBLANES, n_rows, d)
    return pl.pallas_call(
        functools.partial(_pool_kernel, T=T, L=L, tps=tps, G=G),
        grid=(n_rows // T,),
        in_specs=[pl.BlockSpec((T, d), lambda i: (i, 0)), prev, nxt, _row_spec(d),
                  _seg_spec(d, seg_of), _seg_spec(d, seg_of), _seg_spec(d, seg_of),
                  pl.BlockSpec((ng, G, G), lambda i: (0, 0, 0)), _row_spec(d)],
        out_specs=pl.BlockSpec((T, d), lambda i: (i, 0)),
        out_shape=jax.ShapeDtypeStruct((n_rows, d), f32),
        scratch_shapes=[pltpu.VMEM((T + 2 * SUBLANES, d), f32)],
        compiler_params=_params("parallel"),
        name=name,
    )(x, x, x, g, sc, sh, gate, w, scale)


def _cconv_kernel(v_ref, vp_ref, vn_ref, dw_ref, dwb_ref, lng_ref, lnb_ref, w2_ref, b2_ref, x_ref, gate_ref,
                  o_ref, vbuf, cbuf, *, T, tps):
    it = pl.program_id(0) % tps
    hb = CONV_HALO
    d = v_ref.shape[1]
    vbuf[0:hb] = jnp.where(it == 0, 0.0, vp_ref[...])
    vbuf[hb:hb + T] = v_ref[...]
    vbuf[hb + T:2 * hb + T] = jnp.where(it == tps - 1, 0.0, vn_ref[...])
    off = hb - CONV_WIDTH // 2
    rb = min(CONV_ROWS, T)

    def col_body(c, carry):
        c0 = pl.multiple_of(c * LANES, LANES)
        w = dw_ref[:, pl.ds(c0, LANES)]
        for r0 in range(0, T, rb):
            acc = jnp.zeros((rb, LANES), f32)
            for rho in range(SUBLANES):
                taps = [k for k in range(CONV_WIDTH) if (k + off) % SUBLANES == rho]
                span = rb + (max(taps) + off - rho)
                s = vbuf[pl.ds(r0 + rho, span), pl.ds(c0, LANES)]
                for k in taps:
                    q = k + off - rho
                    acc = acc + w[k:k + 1, :] * s[q:q + rb]
            cbuf[pl.ds(r0, rb), pl.ds(c0, LANES)] = acc
        return carry

    lax.fori_loop(0, d // LANES, col_body, 0)
    cv = cbuf[...] + dwb_ref[...]
    xc = cv - jnp.mean(cv, axis=-1, keepdims=True)
    var = jnp.mean(xc * xc, axis=-1, keepdims=True)
    ln = xc * lax.rsqrt(var + EPS) * lng_ref[...] + lnb_ref[...]
    act = (ln * jax.nn.sigmoid(ln)).astype(bf16)
    y = jnp.dot(act, w2_ref[...], preferred_element_type=f32) + b2_ref[...]
    o_ref[...] = x_ref[...] + gate_ref[0] * y


def _cconv_layer(v, dw, dwb, lng, lnb, w2, b2, x, gate, *, L, name):
    n_rows, d = x.shape
    T = _tile(L, CONV_TILE)
    tps = L // T
    nseg = gate.shape[0]
    seg_of = (lambda i: i // tps) if nseg > 1 else (lambda i: 0)
    prev, nxt = _halo_specs(T, CONV_HALO, n_rows, d)
    tile = pl.BlockSpec((T, d), lambda i: (i, 0))
    return pl.pallas_call(
        functools.partial(_cconv_kernel, T=T, tps=tps),
        grid=(n_rows // T,),
        in_specs=[tile, prev, nxt, pl.BlockSpec((CONV_WIDTH, d), lambda i: (0, 0)), _row_spec(d), _row_spec(d),
                  _row_spec(d), pl.BlockSpec((d, d), lambda i: (0, 0)), _row_spec(d), tile, _seg_spec(d, seg_of)],
        out_specs=tile,
        out_shape=jax.ShapeDtypeStruct((n_rows, d), f32),
        scratch_shapes=[pltpu.VMEM((T + 2 * CONV_HALO, d), f32), pltpu.VMEM((T, d), f32)],
        compiler_params=_params("parallel"),
        name=name,
    )(v, v, v, dw, dwb, lng, lnb, w2, b2, x, gate)


def _lru_scan_kernel(xw_ref, xp_ref, xn_ref, cw_ref, cb_ref, wg_ref, ba_ref, bi_ref, lam_ref, h0_ref,
                     s_ref, ht_ref, xbuf, abuf, ubuf, st_ref, *, T, tps, reverse):
    j = pl.program_id(1)
    it = (tps - 1 - j) if reverse else j
    hb = SUBLANES
    d = xw_ref.shape[1]
    blk = d // LRU_HEADS

    @pl.when(j == 0)
    def _():
        st_ref[...] = h0_ref[0]

    xbuf[0:hb] = jnp.where(it == 0, 0.0, xp_ref[...])
    xbuf[hb:hb + T] = xw_ref[...]
    xbuf[hb + T:2 * hb + T] = jnp.where(it == tps - 1, 0.0, xn_ref[...])
    pad_lo = LRU_CONV_WIDTH // 2
    softplus_neg_lam = jax.nn.softplus(-lam_ref[...])
    for hd in range(LRU_HEADS):
        c0 = hd * blk
        xc = cb_ref[:, c0:c0 + blk] + jnp.zeros((T, blk), f32)
        for k in range(LRU_CONV_WIDTH):
            xc = xc + cw_ref[k:k + 1, c0:c0 + blk] * xbuf[hb + k - pad_lo:hb + k - pad_lo + T, c0:c0 + blk]
        z = jnp.dot(xc.astype(bf16), wg_ref[hd], preferred_element_type=f32)
        r = jax.nn.sigmoid(z[:, :blk] + ba_ref[:, c0:c0 + blk])
        ig = jax.nn.sigmoid(z[:, blk:] + bi_ref[:, c0:c0 + blk])
        log_a = -LRU_C * r * softplus_neg_lam[:, c0:c0 + blk]
        a = jnp.exp(log_a)
        one_minus_a2 = -jnp.tanh(log_a) * (a * a + 1.0)
        abuf[:, c0:c0 + blk] = a
        ubuf[:, c0:c0 + blk] = jnp.sqrt(one_minus_a2) * ig * xc

    n_grp = T // SUBLANES

    def grp(gi, h):
        g = (n_grp - 1 - gi) if reverse else gi
        r0 = pl.multiple_of(g * SUBLANES, SUBLANES)
        a = abuf[pl.ds(r0, SUBLANES), :]
        u = ubuf[pl.ds(r0, SUBLANES), :]
        rows = [None] * SUBLANES
        for r in (range(SUBLANES - 1, -1, -1) if reverse else range(SUBLANES)):
            h = a[r:r + 1, :] * h + u[r:r + 1, :]
            rows[r] = h
        s_ref[pl.ds(r0, SUBLANES), :] = jnp.concatenate(rows, axis=0)
        return h

    h_fin = lax.fori_loop(0, n_grp, grp, st_ref[...])
    st_ref[...] = h_fin
    ht_ref[0] = h_fin


def _lru_scan(xw, cw, cb, wg, ba, bi, lam, h0, *, L, reverse, name):
    n_rows, d = xw.shape
    nb = n_rows // L
    T = _tile(L, SEQ_TILE)
    tps = L // T

    def tile_of(b, j):
        return b * tps + ((tps - 1 - j) if reverse else j)

    r = T // SUBLANES
    last = n_rows // SUBLANES - 1
    tile = pl.BlockSpec((T, d), lambda b, j: (tile_of(b, j), 0))
    prev = pl.BlockSpec((SUBLANES, d), lambda b, j: (jnp.maximum(tile_of(b, j) * r - 1, 0), 0))
    nxt = pl.BlockSpec((SUBLANES, d), lambda b, j: (jnp.minimum((tile_of(b, j) + 1) * r, last), 0))
    row = pl.BlockSpec((1, d), lambda b, j: (0, 0))
    state = pl.BlockSpec((1, 1, d), lambda b, j: (b, 0, 0))
    return pl.pallas_call(
        functools.partial(_lru_scan_kernel, T=T, tps=tps, reverse=reverse),
        grid=(nb, tps),
        in_specs=[tile, prev, nxt, pl.BlockSpec((LRU_CONV_WIDTH, d), lambda b, j: (0, 0)), row,
                  pl.BlockSpec(wg.shape, lambda b, j: (0, 0, 0)), row, row, row, state],
        out_specs=[tile, state],
        out_shape=[jax.ShapeDtypeStruct((n_rows, d), f32), jax.ShapeDtypeStruct((nb, 1, d), f32)],
        scratch_shapes=[pltpu.VMEM((T + 2 * SUBLANES, d), f32), pltpu.VMEM((T, d), f32), pltpu.VMEM((T, d), f32),
                        pltpu.VMEM((1, d), f32)],
        compiler_params=_params("parallel", "arbitrary"),
        name=name,
    )(xw, xw, xw, cw, cb, wg, ba, bi, lam, h0)


def _lru_out_kernel(sf_ref, sb_ref, gy_ref, w_ref, x_ref, gate_ref, o_ref):
    y = ((sf_ref[...] + sb_ref[...]) * gy_ref[...]).astype(bf16)
    o_ref[...] = x_ref[...] + gate_ref[0] * jnp.dot(y, w_ref[...], preferred_element_type=f32)


def _lru_out(sf, sb, gy, w_out, x, gate, *, L, name):
    n_rows, d = x.shape
    T = _tile(L, CONV_TILE)
    tps = L // T
    nseg = gate.shape[0]
    seg_of = (lambda i: i // tps) if nseg > 1 else (lambda i: 0)
    tile = pl.BlockSpec((T, d), lambda i: (i, 0))
    return pl.pallas_call(
        _lru_out_kernel,
        grid=(n_rows // T,),
        in_specs=[tile, tile, tile, pl.BlockSpec((d, d), lambda i: (0, 0)), tile, _seg_spec(d, seg_of)],
        out_specs=tile,
        out_shape=jax.ShapeDtypeStruct((n_rows, d), f32),
        compiler_params=_params("parallel"),
        name=name,
    )(sf, sb, gy, w_out, x, gate)


def _ffn_kernel(*refs, T, W, halo, tps, fc, sub, final):
    x_ref = refs[0]
    refs = refs[1:]
    if halo:
        xt_ref, xb_ref = refs[:2]
        refs = refs[2:]
    g_ref, sc_ref, sh_ref, gate_ref, wg_ref, wv_ref, cwg_ref, cwv_ref, cbg_ref, cbv_ref, wd_ref = refs[:11]
    refs = refs[11:]
    if final:
        fg_ref = refs[0]
        refs = refs[1:]
    o_ref, h_ref, ug_ref, uv_ref, act_ref = refs
    i = pl.program_id(0)
    k = pl.program_id(1)
    it = i % tps
    n_rows = T + 2 * halo

    @pl.when(k == 0)
    def _():
        g = g_ref[...]
        sc = sc_ref[0]
        sh = sh_ref[0]
        h_ref[halo:halo + T] = _rms_mod(x_ref[...], g, sc, sh).astype(bf16)
        if halo:
            top = _rms_mod(xt_ref[...], g, sc, sh)
            bot = _rms_mod(xb_ref[...], g, sc, sh)
            h_ref[0:halo] = jnp.where(it == 0, 0.0, top).astype(bf16)
            h_ref[halo + T:n_rows] = jnp.where(it == tps - 1, 0.0, bot).astype(bf16)
        o_ref[...] = jnp.zeros_like(o_ref)

    wpos = lax.broadcasted_iota(jnp.int32, (SUBLANES, 1), 0)

    def shift_rows(p, up):
        rolled = pltpu.roll(p, W - 1 if up else 1, axis=0)
        e0 = W - SUBLANES if up else 0
        edge = jnp.where(wpos == (SUBLANES - 1 if up else 0), 0.0, rolled[e0:e0 + SUBLANES])
        parts = [rolled[:e0], edge] if up else [edge, rolled[SUBLANES:]]
        return jnp.concatenate(parts, axis=0)

    def conv_row(u_ref, cw_ref, cb_ref, r, lanes):
        p = [None] * 3
        for dr in ((0, 1, 2) if halo else (1,)):
            o = (r + dr) * W if halo else r * W
            blk = u_ref[o:o + W, lanes]
            for dw in range(3):
                t = cw_ref[3 * dr + dw:3 * dr + dw + 1, lanes] * blk
                p[dw] = t if p[dw] is None else p[dw] + t
        return p[1] + shift_rows(p[0], False) + shift_rows(p[2], True) + cb_ref[:, lanes]

    for cs in range(0, fc, sub):
        ug_ref[:, cs:cs + sub] = jnp.dot(h_ref[...], wg_ref[:, cs:cs + sub], preferred_element_type=f32)
        uv_ref[:, cs:cs + sub] = jnp.dot(h_ref[...], wv_ref[:, cs:cs + sub], preferred_element_type=f32)
    down = None
    for cs in range(0, fc, sub):
        for c in range(cs, cs + sub, LANES):
            lanes = slice(c, c + LANES)
            for r in range(T // W):
                gv = conv_row(ug_ref, cwg_ref, cbg_ref, r, lanes)
                vv = conv_row(uv_ref, cwv_ref, cbv_ref, r, lanes)
                act_ref[r * W:(r + 1) * W, lanes] = (jax.nn.gelu(gv) * vv).astype(bf16)
        part = jnp.dot(act_ref[:, cs:cs + sub], wd_ref[cs:cs + sub, :], preferred_element_type=f32)
        down = part if down is None else down + part
    o_ref[...] += down

    @pl.when(k == pl.num_programs(1) - 1)
    def _():
        y = x_ref[...] + gate_ref[0] * o_ref[...]
        if final:
            y = y * lax.rsqrt(jnp.mean(y * y, axis=-1, keepdims=True) + EPS) * fg_ref[...]
        o_ref[...] = y


def _ffn_layer(x, g, sc, sh, gate, w_up, cw, cb, w_down, final_g, *, L, two_d, name):
    n_rows, d = x.shape
    F = w_down.shape[0]
    fc = min(FFN_CHUNK, F)
    if two_d:
        W = GRID_W
        T = min(L, FFN_ROWS * W)
        halo = W
    else:
        W = T = L
        halo = 0
    assert L % T == 0 and F % fc == 0
    tps = L // T
    nk = F // fc
    nseg = sc.shape[0]
    seg_of = (lambda i: i // tps) if nseg > 1 else (lambda i: 0)
    sub = min(FFN_SUB, fc)
    tile = pl.BlockSpec((T, d), lambda i, k: (i, 0), pipeline_mode=pl.Buffered(1))
    in_specs = [tile]
    args = [x]
    if halo:
        in_specs += list(_halo_specs(T, halo, n_rows, d))
        args += [x, x]
    in_specs += [_row_spec(d), _seg_spec(d, seg_of), _seg_spec(d, seg_of), _seg_spec(d, seg_of),
                 pl.BlockSpec((d, fc), lambda i, k: (0, k)), pl.BlockSpec((d, fc), lambda i, k: (0, k + nk)),
                 pl.BlockSpec((9, fc), lambda i, k: (0, k)), pl.BlockSpec((9, fc), lambda i, k: (0, k + nk)),
                 pl.BlockSpec((1, fc), lambda i, k: (0, k)), pl.BlockSpec((1, fc), lambda i, k: (0, k + nk)),
                 pl.BlockSpec((fc, d), lambda i, k: (k, 0))]
    args += [g, sc, sh, gate, w_up, w_up, cw, cw, cb, cb, w_down]
    final = final_g is not None
    if final:
        in_specs.append(_row_spec(d))
        args.append(final_g)
    return pl.pallas_call(
        functools.partial(_ffn_kernel, T=T, W=W, halo=halo, tps=tps, fc=fc, sub=sub, final=final),
        grid=(n_rows // T, nk),
        in_specs=in_specs,
        out_specs=tile,
        out_shape=jax.ShapeDtypeStruct((n_rows, d), f32),
        scratch_shapes=[pltpu.VMEM((T + 2 * halo, d), bf16)] + [pltpu.VMEM((T + 2 * halo, fc), f32)] * 2
                       + [pltpu.VMEM((T, fc), bf16)],
        compiler_params=_params("parallel", "arbitrary"),
        name=name,
    )(*args)


def kernel(x, c, ctx, c_ctx, norm1_g, norm2_g, mod_w, mod_b, pool_w, pool_scale, cv_w1, cv_b1, cv_dw, cv_dw_b, cv_ln_g, cv_ln_b, cv_w2, cv_b2, lru_w_x, lru_w_y, lru_conv_w, lru_conv_b, lru_wa, lru_ba, lru_wi, lru_bi, lru_lambda, lru_w_out, ffn_w_up, ffn_conv_w, ffn_conv_b, ffn_w_down, final_g):
    B, L, D = x.shape
    Lc = ctx.shape[1]
    depth = norm1_g.shape[0]
    F2 = ffn_w_up.shape[2]

    c8 = jnp.concatenate([c, c_ctx[None, :], jnp.zeros((SUBLANES - B - 1, D), f32)], axis=0)
    mod = _modulation(c8, mod_w, mod_b).reshape(depth, SUBLANES, N_MOD, 1, D)

    xl = x.reshape(B * L, D)
    xc = ctx.reshape(B * Lc, D)
    row = lambda v: v.reshape(1, -1)

    for i in range(depth):
        kind = i % N_MIXERS
        j = i // N_MIXERS
        ctx_later = any(k % N_MIXERS == 2 for k in range(i + 1, depth))
        ctx_mix = ctx_later or kind == 2
        m_lat = [mod[i, 0:B, q] for q in range(N_MOD)]
        m_ctx = [mod[i, B:B + 1, q] for q in range(N_MOD)]
        streams = [("lat", xl, L, m_lat, True)]
        if ctx_mix:
            streams.append(("ctx", xc, Lc, m_ctx, ctx_later))
        g1n = row(norm1_g[i])
        g2n = row(norm2_g[i])
        new = {}
        if kind == 0:
            pw = pool_w[j].astype(bf16)
            for nm, xs, Ls, m, full in streams:
                if full:
                    new[nm] = _pool_layer(xs, g1n, m[1], m[0], m[2], pw, row(pool_scale[j]), L=Ls,
                                          name=f"pool{i}_{nm}")
        elif kind == 1:
            w1 = cv_w1[j].astype(bf16)
            w2 = cv_w2[j].astype(bf16)
            b1 = row(cv_b1[j])
            for nm, xs, Ls, m, full in streams:
                if full:
                    v = _prenorm_dual(xs, g1n, m[1], m[0], w1, w1, b1, b1, mode="glu", L=Ls,
                                      wb_off=D // min(PROJ_COLS, D), name=f"cv_glu{i}_{nm}")
                    new[nm] = _cconv_layer(v, cv_dw[j], row(cv_dw_b[j]), row(cv_ln_g[j]), row(cv_ln_b[j]), w2,
                                           row(cv_b2[j]), xs, m[2], L=Ls, name=f"cv_tail{i}_{nm}")
        else:
            wx = lru_w_x[j].astype(bf16)
            wy = lru_w_y[j].astype(bf16)
            wo = lru_w_out[j].astype(bf16)
            wg = jnp.concatenate([lru_wa[j], lru_wi[j]], axis=-1).astype(bf16)
            h0 = [jnp.zeros((B, 1, D), f32)] * 2
            for nm, xs, Ls, m, full in reversed(streams):
                xw, gy = _prenorm_dual(xs, g1n, m[1], m[0], wx, wy, None, None, mode="lru", L=Ls,
                                       wb_off=0, name=f"lru_in{i}_{nm}")
                s, ht = [], []
                for dr in range(2):
                    sd, hd = _lru_scan(xw, lru_conv_w[j], row(lru_conv_b[j]), wg[dr], row(lru_ba[j, dr]),
                                       row(lru_bi[j, dr]), row(lru_lambda[j, dr]), h0[dr], L=Ls,
                                       reverse=bool(dr), name=f"lru_scan{i}_{nm}{dr}")
                    s.append(sd)
                    ht.append(hd)
                h0 = ht
                if full:
                    new[nm] = _lru_out(s[0], s[1], gy, wo, xs, m[2], L=Ls, name=f"lru_out{i}_{nm}")
        w_up = ffn_w_up[i].astype(bf16)
        w_dn = ffn_w_down[i].astype(bf16)
        cw = ffn_conv_w[i].reshape(9, F2)
        cb = row(ffn_conv_b[i])
        fin = row(final_g) if i == depth - 1 else None
        xl = _ffn_layer(new["lat"], g2n, m_lat[4], m_lat[3], m_lat[5], w_up, cw, cb, w_dn, fin, L=L, two_d=True,
                        name=f"ffn{i}_lat")
        if ctx_later:
            xc = _ffn_layer(new["ctx"], g2n, m_ctx[4], m_ctx[3], m_ctx[5], w_up, cw, cb, w_dn, None, L=Lc,
                            two_d=False, name=f"ffn{i}_ctx")
    return xl.reshape(B, L, D)
```

```python
import functools

import jax
import jax.numpy as jnp
from jax import lax
from jax.experimental import pallas as pl
from jax.experimental.pallas import tpu as pltpu

GRID_W = 64
N_MIXERS = 3
POOL_WINDOWS = (2, 4, 8, 16)
CONV_WIDTH = 31
LRU_HEADS = 16
LRU_CONV_WIDTH = 4
LRU_C = 8.0
N_MOD = 6
EPS = 1e-6

LANES = 128
SUBLANES = 8
VMEM_LIMIT_BYTES = 56 * 1024 * 1024
SEQ_TILE = 512
FFN_ROWS = 16
FFN_CHUNK = 512
FFN_ROW_GROUPS = 3
CONV_HALO = 16
CONV_ROWS = 64
CONV_TILE = 256
PROJ_COLS = 512
MOD_COLS = 1024

bf16 = jnp.bfloat16
f32 = jnp.float32


def _params(*sem):
    return pltpu.CompilerParams(dimension_semantics=sem, vmem_limit_bytes=VMEM_LIMIT_BYTES)


def _tile(L, pref):
    t = min(L, pref)
    assert L % t == 0 and t % SUBLANES == 0
    return t


def _rms_mod(x, g, sc, sh):
    y = x * lax.rsqrt(jnp.mean(x * x, axis=-1, keepdims=True) + EPS)
    return (y * g) * (1.0 + sc) + sh


def _row_spec(n):
    return pl.BlockSpec((1, n), lambda *_: (0, 0))


def _seg_spec(d, seg_of):
    return pl.BlockSpec((1, 1, d), lambda i, *_: (seg_of(i), 0, 0))


def _halo_specs(T, hb, n_rows, d):
    r = T // hb
    last = n_rows // hb - 1
    prev = pl.BlockSpec((hb, d), lambda i, *_: (jnp.maximum(i * r - 1, 0), 0))
    nxt = pl.BlockSpec((hb, d), lambda i, *_: (jnp.minimum((i + 1) * r, last), 0))
    return prev, nxt


def _mod_kernel(c_ref, w_ref, b_ref, o_ref):
    c = c_ref[...]
    s = (c * jax.nn.sigmoid(c)).astype(bf16)
    o_ref[...] = jnp.dot(s, w_ref[...].astype(bf16), preferred_element_type=f32) + b_ref[...]


def _modulation(c8, mod_w, mod_b):
    depth, d, n = mod_w.shape
    tn = min(MOD_COLS, n)
    return pl.pallas_call(
        _mod_kernel,
        grid=(depth, n // tn),
        in_specs=[pl.BlockSpec((SUBLANES, d), lambda l, j: (0, 0)),
                  pl.BlockSpec((None, d, tn), lambda l, j: (l, 0, j)),
                  pl.BlockSpec((None, 1, tn), lambda l, j: (l, 0, j))],
        out_specs=pl.BlockSpec((None, SUBLANES, tn), lambda l, j: (l, 0, j)),
        out_shape=jax.ShapeDtypeStruct((depth, SUBLANES, n), f32),
        compiler_params=_params("parallel", "parallel"),
        name="modulation",
    )(c8, mod_w, mod_b.reshape(depth, 1, n))


def _prenorm_dual_kernel(*refs, mode, has_bias):
    x_ref, g_ref, sc_ref, sh_ref, wa_ref, wb_ref = refs[:6]
    refs = refs[6:]
    if has_bias:
        ba_ref, bb_ref = refs[:2]
        refs = refs[2:]
    h_ref = refs[-1]
    outs = refs[:-1]

    @pl.when(pl.program_id(1) == 0)
    def _():
        h_ref[...] = _rms_mod(x_ref[...], g_ref[...], sc_ref[0], sh_ref[0]).astype(bf16)

    h = h_ref[...]
    a = jnp.dot(h, wa_ref[...], preferred_element_type=f32)
    b = jnp.dot(h, wb_ref[...], preferred_element_type=f32)
    if has_bias:
        a = a + ba_ref[...]
        b = b + bb_ref[...]
    if mode == "glu":
        outs[0][...] = a * jax.nn.sigmoid(b)
    else:
        outs[0][...] = a
        outs[1][...] = jax.nn.gelu(b)


def _prenorm_dual(x, g, sc, sh, wa, wb, ba, bb, *, mode, L, wb_off, name):
    n_rows, d = x.shape
    T = _tile(L, SEQ_TILE)
    tps = L // T
    nseg = sc.shape[0]
    seg_of = (lambda i: i // tps) if nseg > 1 else (lambda i: 0)
    n_out = d
    tn = min(PROJ_COLS, n_out)
    has_bias = ba is not None
    in_specs = [pl.BlockSpec((T, d), lambda i, j: (i, 0)), _row_spec(d), _seg_spec(d, seg_of), _seg_spec(d, seg_of),
                pl.BlockSpec((d, tn), lambda i, j: (0, j)),
                pl.BlockSpec((d, tn), lambda i, j: (0, j + wb_off))]
    args = [x, g, sc, sh, wa, wb]
    if has_bias:
        in_specs += [pl.BlockSpec((1, tn), lambda i, j: (0, j)), pl.BlockSpec((1, tn), lambda i, j: (0, j + wb_off))]
        args += [ba, bb]
    o_spec = pl.BlockSpec((T, tn), lambda i, j: (i, j))
    o_shape = jax.ShapeDtypeStruct((n_rows, n_out), f32)
    n_o = 1 if mode == "glu" else 2
    res = pl.pallas_call(
        functools.partial(_prenorm_dual_kernel, mode=mode, has_bias=has_bias),
        grid=(n_rows // T, n_out // tn),
        in_specs=in_specs,
        out_specs=[o_spec] * n_o,
        out_shape=[o_shape] * n_o,
        scratch_shapes=[pltpu.VMEM((T, d), bf16)],
        compiler_params=_params("parallel", "arbitrary"),
        name=name,
    )(*args)
    return res[0] if n_o == 1 else res


def _pool_kernel(x_ref, xp_ref, xn_ref, g_ref, sc_ref, sh_ref, gate_ref, w_ref, scale_ref, o_ref, hbuf,
                 *, T, L, tps, G):
    it = pl.program_id(0) % tps
    g = g_ref[...]
    sc = sc_ref[0]
    sh = sh_ref[0]
    gate = gate_ref[0]
    hb = SUBLANES
    x = x_ref[...]
    h = _rms_mod(x, g, sc, sh)
    hbuf[0:hb] = jnp.where(it == 0, 0.0, _rms_mod(xp_ref[...], g, sc, sh))
    hbuf[hb:hb + T] = h
    hbuf[hb + T:2 * hb + T] = jnp.where(it == tps - 1, 0.0, _rms_mod(xn_ref[...], g, sc, sh))
    tpos = it * T + lax.broadcasted_iota(jnp.int32, (T, 1), 0)
    for gi, win in enumerate(POOL_WINDOWS):
        half = win // 2
        c0 = gi * G
        s = hbuf[hb - half:hb - half + T, c0:c0 + G]
        for o in range(-half + 1, half):
            s = s + hbuf[hb + o:hb + o + T, c0:c0 + G]
        cnt = (jnp.minimum(tpos + half, L) - jnp.maximum(tpos - half, 0)).astype(f32)
        dlt = (s / cnt - h[:, c0:c0 + G]).astype(bf16)
        y = jnp.dot(dlt, w_ref[gi], preferred_element_type=f32) * scale_ref[:, c0:c0 + G]
        o_ref[:, c0:c0 + G] = x[:, c0:c0 + G] + gate[:, c0:c0 + G] * y


def _pool_layer(x, g, sc, sh, gate, w, scale, *, L, name):
    n_rows, d = x.shape
    T = _tile(L, SEQ_TILE)
    tps = L // T
    nseg = sc.shape[0]
    seg_of = (lambda i: i // tps) if nseg > 1 else (lambda i: 0)
    ng, G, _ = w.shape
    assert max(POOL_WINDOWS) // 2 <= SUBLANES
    prev, nxt = _halo_specs(T, SUBLANES, n_rows, d)
    return pl.pallas_call(
        functools.partial(_pool_kernel, T=T, L=L, tps=tps, G=G),
        grid=(n_rows // T,),
        in_specs=[pl.BlockSpec((T, d), lambda i: (i, 0)), prev, nxt, _row_spec(d),
                  _seg_spec(d, seg_of), _seg_spec(d, seg_of), _seg_spec(d, seg_of),
                  pl.BlockSpec((ng, G, G), lambda i: (0, 0, 0)), _row_spec(d)],
        out_specs=pl.BlockSpec((T, d), lambda i: (i, 0)),
        out_shape=jax.ShapeDtypeStruct((n_rows, d), f32),
        scratch_shapes=[pltpu.VMEM((T + 2 * SUBLANES, d), f32)],
        compiler_params=_params("parallel"),
        name=name,
    )(x, x, x, g, sc, sh, gate, w, scale)


def _cconv_kernel(v_ref, vp_ref, vn_ref, dw_ref, dwb_ref, lng_ref, lnb_ref, w2_ref, b2_ref, x_ref, gate_ref,
                  o_ref, vbuf, cbuf, *, T, tps):
    it = pl.program_id(0) % tps
    hb = CONV_HALO
    d = v_ref.shape[1]
    vbuf[0:hb] = jnp.where(it == 0, 0.0, vp_ref[...])
    vbuf[hb:hb + T] = v_ref[...]
    vbuf[hb + T:2 * hb + T] = jnp.where(it == tps - 1, 0.0, vn_ref[...])
    off = hb - CONV_WIDTH // 2
    rb = min(CONV_ROWS, T)

    def col_body(c, carry):
        c0 = pl.multiple_of(c * LANES, LANES)
        w = dw_ref[:, pl.ds(c0, LANES)]
        for r0 in range(0, T, rb):
            acc = jnp.zeros((rb, LANES), f32)
            span = rb + 2 * hb
            a = vbuf[pl.ds(r0, span), pl.ds(c0, LANES)]
            for rho in range(SUBLANES):
                s = a if rho == 0 else pltpu.roll(a, span - rho, axis=0)
                for k in range(CONV_WIDTH):
                    if (k + off) % SUBLANES == rho:
                        q = k + off - rho
                        acc = acc + w[k:k + 1, :] * s[q:q + rb]
            cbuf[pl.ds(r0, rb), pl.ds(c0, LANES)] = acc
        return carry

    lax.fori_loop(0, d // LANES, col_body, 0)
    cv = cbuf[...] + dwb_ref[...]
    xc = cv - jnp.mean(cv, axis=-1, keepdims=True)
    var = jnp.mean(xc * xc, axis=-1, keepdims=True)
    ln = xc * lax.rsqrt(var + EPS) * lng_ref[...] + lnb_ref[...]
    act = (ln * jax.nn.sigmoid(ln)).astype(bf16)
    y = jnp.dot(act, w2_ref[...], preferred_element_type=f32) + b2_ref[...]
    o_ref[...] = x_ref[...] + gate_ref[0] * y


def _cconv_layer(v, dw, dwb, lng, lnb, w2, b2, x, gate, *, L, name):
    n_rows, d = x.shape
    T = _tile(L, CONV_TILE)
    tps = L // T
    nseg = gate.shape[0]
    seg_of = (lambda i: i // tps) if nseg > 1 else (lambda i: 0)
    prev, nxt = _halo_specs(T, CONV_HALO, n_rows, d)
    tile = pl.BlockSpec((T, d), lambda i: (i, 0))
    return pl.pallas_call(
        functools.partial(_cconv_kernel, T=T, tps=tps),
        grid=(n_rows // T,),
        in_specs=[tile, prev, nxt, pl.BlockSpec((CONV_WIDTH, d), lambda i: (0, 0)), _row_spec(d), _row_spec(d),
                  _row_spec(d), pl.BlockSpec((d, d), lambda i: (0, 0)), _row_spec(d), tile, _seg_spec(d, seg_of)],
        out_specs=tile,
        out_shape=jax.ShapeDtypeStruct((n_rows, d), f32),
        scratch_shapes=[pltpu.VMEM((T + 2 * CONV_HALO, d), f32), pltpu.VMEM((T, d), f32)],
        compiler_params=_params("parallel"),
        name=name,
    )(v, v, v, dw, dwb, lng, lnb, w2, b2, x, gate)


def _lru_scan_kernel(xw_ref, xp_ref, xn_ref, cw_ref, cb_ref, wg_ref, ba_ref, bi_ref, lam_ref, h0_ref,
                     s_ref, ht_ref, xbuf, abuf, ubuf, st_ref, *, T, tps, reverse):
    j = pl.program_id(1)
    it = (tps - 1 - j) if reverse else j
    hb = SUBLANES
    d = xw_ref.shape[1]
    blk = d // LRU_HEADS

    @pl.when(j == 0)
    def _():
        st_ref[...] = h0_ref[0]

    xbuf[0:hb] = jnp.where(it == 0, 0.0, xp_ref[...])
    xbuf[hb:hb + T] = xw_ref[...]
    xbuf[hb + T:2 * hb + T] = jnp.where(it == tps - 1, 0.0, xn_ref[...])
    pad_lo = LRU_CONV_WIDTH // 2
    softplus_neg_lam = jax.nn.softplus(-lam_ref[...])
    for hd in range(LRU_HEADS):
        c0 = hd * blk
        xc = cb_ref[:, c0:c0 + blk] + jnp.zeros((T, blk), f32)
        for k in range(LRU_CONV_WIDTH):
            xc = xc + cw_ref[k:k + 1, c0:c0 + blk] * xbuf[hb + k - pad_lo:hb + k - pad_lo + T, c0:c0 + blk]
        z = jnp.dot(xc.astype(bf16), wg_ref[hd], preferred_element_type=f32)
        r = jax.nn.sigmoid(z[:, :blk] + ba_ref[:, c0:c0 + blk])
        ig = jax.nn.sigmoid(z[:, blk:] + bi_ref[:, c0:c0 + blk])
        log_a = -LRU_C * r * softplus_neg_lam[:, c0:c0 + blk]
        a = jnp.exp(log_a)
        one_minus_a2 = -jnp.tanh(log_a) * (a * a + 1.0)
        abuf[:, c0:c0 + blk] = a
        ubuf[:, c0:c0 + blk] = jnp.sqrt(one_minus_a2) * ig * xc

    n_grp = T // SUBLANES

    def grp(gi, h):
        g = (n_grp - 1 - gi) if reverse else gi
        r0 = pl.multiple_of(g * SUBLANES, SUBLANES)
        a = abuf[pl.ds(r0, SUBLANES), :]
        u = ubuf[pl.ds(r0, SUBLANES), :]
        rows = [None] * SUBLANES
        for r in (range(SUBLANES - 1, -1, -1) if reverse else range(SUBLANES)):
            h = a[r:r + 1, :] * h + u[r:r + 1, :]
            rows[r] = h
        s_ref[pl.ds(r0, SUBLANES), :] = jnp.concatenate(rows, axis=0)
        return h

    h_fin = lax.fori_loop(0, n_grp, grp, st_ref[...])
    st_ref[...] = h_fin
    ht_ref[0] = h_fin


def _lru_scan(xw, cw, cb, wg, ba, bi, lam, h0, *, L, reverse, name):
    n_rows, d = xw.shape
    nb = n_rows // L
    T = _tile(L, SEQ_TILE)
    tps = L // T

    def tile_of(b, j):
        return b * tps + ((tps - 1 - j) if reverse else j)

    r = T // SUBLANES
    last = n_rows // SUBLANES - 1
    tile = pl.BlockSpec((T, d), lambda b, j: (tile_of(b, j), 0))
    prev = pl.BlockSpec((SUBLANES, d), lambda b, j: (jnp.maximum(tile_of(b, j) * r - 1, 0), 0))
    nxt = pl.BlockSpec((SUBLANES, d), lambda b, j: (jnp.minimum((tile_of(b, j) + 1) * r, last), 0))
    row = pl.BlockSpec((1, d), lambda b, j: (0, 0))
    state = pl.BlockSpec((1, 1, d), lambda b, j: (b, 0, 0))
    return pl.pallas_call(
        functools.partial(_lru_scan_kernel, T=T, tps=tps, reverse=reverse),
        grid=(nb, tps),
        in_specs=[tile, prev, nxt, pl.BlockSpec((LRU_CONV_WIDTH, d), lambda b, j: (0, 0)), row,
                  pl.BlockSpec(wg.shape, lambda b, j: (0, 0, 0)), row, row, row, state],
        out_specs=[tile, state],
        out_shape=[jax.ShapeDtypeStruct((n_rows, d), f32), jax.ShapeDtypeStruct((nb, 1, d), f32)],
        scratch_shapes=[pltpu.VMEM((T + 2 * SUBLANES, d), f32), pltpu.VMEM((T, d), f32), pltpu.VMEM((T, d), f32),
                        pltpu.VMEM((1, d), f32)],
        compiler_params=_params("parallel", "arbitrary"),
        name=name,
    )(xw, xw, xw, cw, cb, wg, ba, bi, lam, h0)


def _lru_out_kernel(sf_ref, sb_ref, gy_ref, w_ref, x_ref, gate_ref, o_ref):
    y = ((sf_ref[...] + sb_ref[...]) * gy_ref[...]).astype(bf16)
    o_ref[...] = x_ref[...] + gate_ref[0] * jnp.dot(y, w_ref[...], preferred_element_type=f32)


def _lru_out(sf, sb, gy, w_out, x, gate, *, L, name):
    n_rows, d = x.shape
    T = _tile(L, CONV_TILE)
    tps = L // T
    nseg = gate.shape[0]
    seg_of = (lambda i: i // tps) if nseg > 1 else (lambda i: 0)
    tile = pl.BlockSpec((T, d), lambda i: (i, 0))
    return pl.pallas_call(
        _lru_out_kernel,
        grid=(n_rows // T,),
        in_specs=[tile, tile, tile, pl.BlockSpec((d, d), lambda i: (0, 0)), tile, _seg_spec(d, seg_of)],
        out_specs=tile,
        out_shape=jax.ShapeDtypeStruct((n_rows, d), f32),
        compiler_params=_params("parallel"),
        name=name,
    )(sf, sb, gy, w_out, x, gate)


def _ffn_kernel(*refs, T, W, halo, tps, fc, final):
    x_ref = refs[0]
    refs = refs[1:]
    if halo:
        xt_ref, xb_ref = refs[:2]
        refs = refs[2:]
    g_ref, sc_ref, sh_ref, gate_ref, wg_ref, wv_ref, cwg_ref, cwv_ref, cbg_ref, cbv_ref, wd_ref = refs[:11]
    refs = refs[11:]
    if final:
        fg_ref = refs[0]
        refs = refs[1:]
    o_ref, h_ref, ug_ref, uv_ref, act_ref = refs
    i = pl.program_id(0)
    k = pl.program_id(1)
    it = i % tps
    n_rows = T + 2 * halo

    @pl.when(k == 0)
    def _():
        g = g_ref[...]
        sc = sc_ref[0]
        sh = sh_ref[0]
        h_ref[halo:halo + T] = _rms_mod(x_ref[...], g, sc, sh).astype(bf16)
        if halo:
            top = _rms_mod(xt_ref[...], g, sc, sh)
            bot = _rms_mod(xb_ref[...], g, sc, sh)
            h_ref[0:halo] = jnp.where(it == 0, 0.0, top).astype(bf16)
            h_ref[halo + T:n_rows] = jnp.where(it == tps - 1, 0.0, bot).astype(bf16)
        o_ref[...] = jnp.zeros_like(o_ref)

    wpos = lax.broadcasted_iota(jnp.int32, (SUBLANES, 1), 0)

    def shift_rows(p, up):
        rolled = pltpu.roll(p, W - 1 if up else 1, axis=0)
        e0 = W - SUBLANES if up else 0
        edge = jnp.where(wpos == (SUBLANES - 1 if up else 0), 0.0, rolled[e0:e0 + SUBLANES])
        parts = [rolled[:e0], edge] if up else [edge, rolled[SUBLANES:]]
        return jnp.concatenate(parts, axis=0)

    def conv_row(u_ref, cw_ref, cb_ref, r, lanes):
        p = [None] * 3
        for dr in ((0, 1, 2) if halo else (1,)):
            o = (r + dr) * W if halo else r * W
            blk = u_ref[o:o + W, lanes]
            for dw in range(3):
                t = cw_ref[3 * dr + dw:3 * dr + dw + 1, lanes] * blk
                p[dw] = t if p[dw] is None else p[dw] + t
        return p[1] + shift_rows(p[0], False) + shift_rows(p[2], True) + cb_ref[:, lanes]

    n_out = T // W
    hr = n_out + (2 if halo else 0)
    n_grp = next(gn for gn in (FFN_ROW_GROUPS, 2, 1) if hr % gn == 0)
    gsz = hr // n_grp
    for m in range(n_grp):
        rows = slice(m * gsz * W, (m + 1) * gsz * W)
        ug_ref[rows] = jnp.dot(h_ref[rows], wg_ref[...], preferred_element_type=f32)
        uv_ref[rows] = jnp.dot(h_ref[rows], wv_ref[...], preferred_element_type=f32)
    done = 0
    for m in range(n_grp):
        last = n_out if m == n_grp - 1 else max(done, min(n_out, (m + 1) * gsz - (2 if halo else 0)))
        if last == done:
            continue
        for c in range(0, fc, LANES):
            lanes = slice(c, c + LANES)
            for r in range(done, last):
                gv = conv_row(ug_ref, cwg_ref, cbg_ref, r, lanes)
                vv = conv_row(uv_ref, cwv_ref, cbv_ref, r, lanes)
                act_ref[r * W:(r + 1) * W, lanes] = (jax.nn.gelu(gv) * vv).astype(bf16)
        orows = slice(done * W, last * W)
        o_ref[orows] += jnp.dot(act_ref[orows], wd_ref[...], preferred_element_type=f32)
        done = last

    @pl.when(k == pl.num_programs(1) - 1)
    def _():
        y = x_ref[...] + gate_ref[0] * o_ref[...]
        if final:
            y = y * lax.rsqrt(jnp.mean(y * y, axis=-1, keepdims=True) + EPS) * fg_ref[...]
        o_ref[...] = y


def _ffn_layer(x, g, sc, sh, gate, w_up, cw, cb, w_down, final_g, *, layer, L, two_d, name):
    n_rows, d = x.shape
    F = w_down.shape[1]
    fc = min(FFN_CHUNK, F)
    nseg = sc.shape[0]
    if two_d:
        W = GRID_W
        T = min(L, FFN_ROWS * W)
        halo = W
        tps = L // T
        assert L % T == 0
    else:
        assert nseg == 1
        W = L
        T = n_rows
        halo = 0
        tps = 1
    assert F % fc == 0
    nk = F // fc
    seg_of = (lambda i: i // tps) if nseg > 1 else (lambda i: 0)
    tile = pl.BlockSpec((T, d), lambda i, k: (i, 0), pipeline_mode=pl.Buffered(1))
    in_specs = [tile]
    args = [x]
    if halo:
        in_specs += list(_halo_specs(T, halo, n_rows, d))
        args += [x, x]
    in_specs += [_row_spec(d), _seg_spec(d, seg_of), _seg_spec(d, seg_of), _seg_spec(d, seg_of),
                 pl.BlockSpec((None, d, fc), lambda i, k: (layer, 0, k)),
                 pl.BlockSpec((None, d, fc), lambda i, k: (layer, 0, k + nk)),
                 pl.BlockSpec((None, 9, fc), lambda i, k: (layer, 0, k)),
                 pl.BlockSpec((None, 9, fc), lambda i, k: (layer, 0, k + nk)),
                 pl.BlockSpec((None, 1, fc), lambda i, k: (layer, 0, k)),
                 pl.BlockSpec((None, 1, fc), lambda i, k: (layer, 0, k + nk)),
                 pl.BlockSpec((None, fc, d), lambda i, k: (layer, k, 0))]
    args += [g, sc, sh, gate, w_up, w_up, cw, cw, cb, cb, w_down]
    final = final_g is not None
    if final:
        in_specs.append(_row_spec(d))
        args.append(final_g)
    return pl.pallas_call(
        functools.partial(_ffn_kernel, T=T, W=W, halo=halo, tps=tps, fc=fc, final=final),
        grid=(n_rows // T, nk),
        in_specs=in_specs,
        out_specs=tile,
        out_shape=jax.ShapeDtypeStruct((n_rows, d), f32),
        scratch_shapes=[pltpu.VMEM((T + 2 * halo, d), bf16)] + [pltpu.VMEM((T + 2 * halo, fc), f32)] * 2
                       + [pltpu.VMEM((T, fc), bf16)],
        compiler_params=_params("parallel", "arbitrary"),
        name=name,
    )(*args)


def kernel(x, c, ctx, c_ctx, norm1_g, norm2_g, mod_w, mod_b, pool_w, pool_scale, cv_w1, cv_b1, cv_dw, cv_dw_b, cv_ln_g, cv_ln_b, cv_w2, cv_b2, lru_w_x, lru_w_y, lru_conv_w, lru_conv_b, lru_wa, lru_ba, lru_wi, lru_bi, lru_lambda, lru_w_out, ffn_w_up, ffn_conv_w, ffn_conv_b, ffn_w_down, final_g):
    B, L, D = x.shape
    Lc = ctx.shape[1]
    depth = norm1_g.shape[0]
    F2 = ffn_w_up.shape[2]

    c8 = jnp.concatenate([c, c_ctx[None, :], jnp.zeros((SUBLANES - B - 1, D), f32)], axis=0)
    mod = _modulation(c8, mod_w, mod_b).reshape(depth, SUBLANES, N_MOD, 1, D)

    xl = x.reshape(B * L, D)
    xc = ctx.reshape(B * Lc, D)
    row = lambda v: v.reshape(1, -1)
    w_up = ffn_w_up.astype(bf16)
    w_dn = ffn_w_down.astype(bf16)
    cw = ffn_conv_w.reshape(depth, 9, F2)
    cb = ffn_conv_b.reshape(depth, 1, F2)

    for i in range(depth):
        kind = i % N_MIXERS
        j = i // N_MIXERS
        ctx_later = any(k % N_MIXERS == 2 for k in range(i + 1, depth))
        ctx_mix = ctx_later or kind == 2
        m_lat = [mod[i, 0:B, q] for q in range(N_MOD)]
        m_ctx = [mod[i, B:B + 1, q] for q in range(N_MOD)]
        streams = [("lat", xl, L, m_lat, True)]
        if ctx_mix:
            streams.append(("ctx", xc, Lc, m_ctx, ctx_later))
        g1n = row(norm1_g[i])
        g2n = row(norm2_g[i])
        new = {}
        if kind == 0:
            pw = pool_w[j].astype(bf16)
            for nm, xs, Ls, m, full in streams:
                if full:
                    new[nm] = _pool_layer(xs, g1n, m[1], m[0], m[2], pw, row(pool_scale[j]), L=Ls,
                                          name=f"pool{i}_{nm}")
        elif kind == 1:
            w1 = cv_w1[j].astype(bf16)
            w2 = cv_w2[j].astype(bf16)
            b1 = row(cv_b1[j])
            for nm, xs, Ls, m, full in streams:
                if full:
                    v = _prenorm_dual(xs, g1n, m[1], m[0], w1, w1, b1, b1, mode="glu", L=Ls,
                                      wb_off=D // min(PROJ_COLS, D), name=f"cv_glu{i}_{nm}")
                    new[nm] = _cconv_layer(v, cv_dw[j], row(cv_dw_b[j]), row(cv_ln_g[j]), row(cv_ln_b[j]), w2,
                                           row(cv_b2[j]), xs, m[2], L=Ls, name=f"cv_tail{i}_{nm}")
        else:
            wx = lru_w_x[j].astype(bf16)
            wy = lru_w_y[j].astype(bf16)
            wo = lru_w_out[j].astype(bf16)
            wg = jnp.concatenate([lru_wa[j], lru_wi[j]], axis=-1).astype(bf16)
            h0 = [jnp.zeros((B, 1, D), f32)] * 2
            for nm, xs, Ls, m, full in reversed(streams):
                xw, gy = _prenorm_dual(xs, g1n, m[1], m[0], wx, wy, None, None, mode="lru", L=Ls,
                                       wb_off=0, name=f"lru_in{i}_{nm}")
                s, ht = [], []
                for dr in range(2):
                    sd, hd = _lru_scan(xw, lru_conv_w[j], row(lru_conv_b[j]), wg[dr], row(lru_ba[j, dr]),
                                       row(lru_bi[j, dr]), row(lru_lambda[j, dr]), h0[dr], L=Ls,
                                       reverse=bool(dr), name=f"lru_scan{i}_{nm}{dr}")
                    s.append(sd)
                    ht.append(hd)
                h0 = ht
                if full:
                    new[nm] = _lru_out(s[0], s[1], gy, wo, xs, m[2], L=Ls, name=f"lru_out{i}_{nm}")
        fin = row(final_g) if i == depth - 1 else None
        xl = _ffn_layer(new["lat"], g2n, m_lat[4], m_lat[3], m_lat[5], w_up, cw, cb, w_dn, fin, layer=i, L=L,
                        two_d=True, name=f"ffn{i}_lat")
        if ctx_later:
            xc = _ffn_layer(new["ctx"], g2n, m_ctx[4], m_ctx[3], m_ctx[5], w_up, cw, cb, w_dn, None, layer=i,
                            L=Lc, two_d=False, name=f"ffn{i}_ctx")
    return xl.reshape(B, L, D)
```

```python
import functools

import jax
import jax.numpy as jnp
from jax import lax
from jax.experimental import pallas as pl
from jax.experimental.pallas import tpu as pltpu

GRID_W = 64
N_MIXERS = 3
POOL_WINDOWS = (2, 4, 8, 16)
CONV_WIDTH = 31
LRU_HEADS = 16
LRU_CONV_WIDTH = 4
LRU_C = 8.0
N_MOD = 6
EPS = 1e-6

LANES = 128
SUBLANES = 8
VMEM_LIMIT_BYTES = 56 * 1024 * 1024
SEQ_TILE = 512
FFN_ROWS = 16
FFN_CHUNK = 512
FFN_EDGE_ROWS = 64
FFN_ROW_GROUPS = 3
CONV_HALO = 16
CONV_ROWS = 64
CONV_TILE = 256
PROJ_COLS = 512
MOD_COLS = 1024

bf16 = jnp.bfloat16
f32 = jnp.float32


def _params(*sem):
    return pltpu.CompilerParams(dimension_semantics=sem, vmem_limit_bytes=VMEM_LIMIT_BYTES)


def _tile(L, pref):
    t = min(L, pref)
    assert L % t == 0 and t % SUBLANES == 0
    return t


def _rms_mod(x, g, sc, sh):
    y = x * lax.rsqrt(jnp.mean(x * x, axis=-1, keepdims=True) + EPS)
    return (y * g) * (1.0 + sc) + sh


def _row_spec(n):
    return pl.BlockSpec((1, n), lambda *_: (0, 0))


def _seg_spec(d, seg_of):
    return pl.BlockSpec((1, 1, d), lambda i, *_: (seg_of(i), 0, 0))


def _halo_specs(T, hb, n_rows, d):
    r = T // hb
    last = n_rows // hb - 1
    prev = pl.BlockSpec((hb, d), lambda i, *_: (jnp.maximum(i * r - 1, 0), 0))
    nxt = pl.BlockSpec((hb, d), lambda i, *_: (jnp.minimum((i + 1) * r, last), 0))
    return prev, nxt


def _mod_kernel(c_ref, w_ref, b_ref, o_ref):
    c = c_ref[...]
    s = (c * jax.nn.sigmoid(c)).astype(bf16)
    o_ref[...] = jnp.dot(s, w_ref[...].astype(bf16), preferred_element_type=f32) + b_ref[...]


def _modulation(c8, mod_w, mod_b):
    depth, d, n = mod_w.shape
    tn = min(MOD_COLS, n)
    return pl.pallas_call(
        _mod_kernel,
        grid=(depth, n // tn),
        in_specs=[pl.BlockSpec((SUBLANES, d), lambda l, j: (0, 0)),
                  pl.BlockSpec((None, d, tn), lambda l, j: (l, 0, j)),
                  pl.BlockSpec((None, 1, tn), lambda l, j: (l, 0, j))],
        out_specs=pl.BlockSpec((None, SUBLANES, tn), lambda l, j: (l, 0, j)),
        out_shape=jax.ShapeDtypeStruct((depth, SUBLANES, n), f32),
        compiler_params=_params("parallel", "parallel"),
        name="modulation",
    )(c8, mod_w, mod_b.reshape(depth, 1, n))


def _prenorm_dual_kernel(*refs, mode, has_bias):
    x_ref, g_ref, sc_ref, sh_ref, wa_ref, wb_ref = refs[:6]
    refs = refs[6:]
    if has_bias:
        ba_ref, bb_ref = refs[:2]
        refs = refs[2:]
    h_ref = refs[-1]
    outs = refs[:-1]

    @pl.when(pl.program_id(1) == 0)
    def _():
        h_ref[...] = _rms_mod(x_ref[...], g_ref[...], sc_ref[0], sh_ref[0]).astype(bf16)

    h = h_ref[...]
    a = jnp.dot(h, wa_ref[...], preferred_element_type=f32)
    b = jnp.dot(h, wb_ref[...], preferred_element_type=f32)
    if has_bias:
        a = a + ba_ref[...]
        b = b + bb_ref[...]
    if mode == "glu":
        outs[0][...] = a * jax.nn.sigmoid(b)
    else:
        outs[0][...] = a
        outs[1][...] = jax.nn.gelu(b)


def _prenorm_dual(x, g, sc, sh, wa, wb, ba, bb, *, mode, L, wb_off, name):
    n_rows, d = x.shape
    T = _tile(L, SEQ_TILE)
    tps = L // T
    nseg = sc.shape[0]
    seg_of = (lambda i: i // tps) if nseg > 1 else (lambda i: 0)
    n_out = d
    tn = min(PROJ_COLS, n_out)
    has_bias = ba is not None
    in_specs = [pl.BlockSpec((T, d), lambda i, j: (i, 0)), _row_spec(d), _seg_spec(d, seg_of), _seg_spec(d, seg_of),
                pl.BlockSpec((d, tn), lambda i, j: (0, j)),
                pl.BlockSpec((d, tn), lambda i, j: (0, j + wb_off))]
    args = [x, g, sc, sh, wa, wb]
    if has_bias:
        in_specs += [pl.BlockSpec((1, tn), lambda i, j: (0, j)), pl.BlockSpec((1, tn), lambda i, j: (0, j + wb_off))]
        args += [ba, bb]
    o_spec = pl.BlockSpec((T, tn), lambda i, j: (i, j))
    o_shape = jax.ShapeDtypeStruct((n_rows, n_out), f32)
    n_o = 1 if mode == "glu" else 2
    res = pl.pallas_call(
        functools.partial(_prenorm_dual_kernel, mode=mode, has_bias=has_bias),
        grid=(n_rows // T, n_out // tn),
        in_specs=in_specs,
        out_specs=[o_spec] * n_o,
        out_shape=[o_shape] * n_o,
        scratch_shapes=[pltpu.VMEM((T, d), bf16)],
        compiler_params=_params("parallel", "arbitrary"),
        name=name,
    )(*args)
    return res[0] if n_o == 1 else res


def _pool_kernel(x_ref, xp_ref, xn_ref, g_ref, sc_ref, sh_ref, gate_ref, w_ref, scale_ref, o_ref, hbuf,
                 *, T, L, tps, G):
    it = pl.program_id(0) % tps
    g = g_ref[...]
    sc = sc_ref[0]
    sh = sh_ref[0]
    gate = gate_ref[0]
    hb = SUBLANES
    x = x_ref[...]
    h = _rms_mod(x, g, sc, sh)
    hbuf[0:hb] = jnp.where(it == 0, 0.0, _rms_mod(xp_ref[...], g, sc, sh))
    hbuf[hb:hb + T] = h
    hbuf[hb + T:2 * hb + T] = jnp.where(it == tps - 1, 0.0, _rms_mod(xn_ref[...], g, sc, sh))
    tpos = it * T + lax.broadcasted_iota(jnp.int32, (T, 1), 0)
    for gi, win in enumerate(POOL_WINDOWS):
        half = win // 2
        c0 = gi * G
        n = T + 2 * hb
        shift = lambda a, d: pltpu.roll(a, (-d) % n, axis=0)
        w = hbuf[:, c0:c0 + G]
        w = shift(w, -1) + w
        span = 2
        while span < win:
            w = shift(w, -(span // 2)) + shift(w, span // 2)
            span *= 2
        s = w[hb:hb + T]
        cnt = (jnp.minimum(tpos + half, L) - jnp.maximum(tpos - half, 0)).astype(f32)
        dlt = (s / cnt - h[:, c0:c0 + G]).astype(bf16)
        y = jnp.dot(dlt, w_ref[gi], preferred_element_type=f32) * scale_ref[:, c0:c0 + G]
        o_ref[:, c0:c0 + G] = x[:, c0:c0 + G] + gate[:, c0:c0 + G] * y


def _pool_layer(x, g, sc, sh, gate, w, scale, *, L, name):
    n_rows, d = x.shape
    T = _tile(L, SEQ_TILE)
    tps = L // T
    nseg = sc.shape[0]
    seg_of = (lambda i: i // tps) if nseg > 1 else (lambda i: 0)
    ng, G, _ = w.shape
    assert max(POOL_WINDOWS) // 2 <= SUBLANES and all(w >= 2 and w & (w - 1) == 0 for w in POOL_WINDOWS)
    prev, nxt = _halo_specs(T, SUBLANES, n_rows, d)
    return pl.pallas_call(
        functools.partial(_pool_kernel, T=T, L=L, tps=tps, G=G),
        grid=(n_rows // T,),
        in_specs=[pl.BlockSpec((T, d), lambda i: (i, 0)), prev, nxt, _row_spec(d),
                  _seg_spec(d, seg_of), _seg_spec(d, seg_of), _seg_spec(d, seg_of),
                  pl.BlockSpec((ng, G, G), lambda i: (0, 0, 0)), _row_spec(d)],
        out_specs=pl.BlockSpec((T, d), lambda i: (i, 0)),
        out_shape=jax.ShapeDtypeStruct((n_rows, d), f32),
        scratch_shapes=[pltpu.VMEM((T + 2 * SUBLANES, d), f32)],
        compiler_params=_params("parallel"),
        name=name,
    )(x, x, x, g, sc, sh, gate, w, scale)


def _cconv_kernel(v_ref, vp_ref, vn_ref, dw_ref, dwb_ref, lng_ref, lnb_ref, w2_ref, b2_ref, x_ref, gate_ref,
                  o_ref, vbuf, cbuf, *, T, tps):
    it = pl.program_id(0) % tps
    hb = CONV_HALO
    d = v_ref.shape[1]
    vbuf[0:hb] = jnp.where(it == 0, 0.0, vp_ref[...])
    vbuf[hb:hb + T] = v_ref[...]
    vbuf[hb + T:2 * hb + T] = jnp.where(it == tps - 1, 0.0, vn_ref[...])
    off = hb - CONV_WIDTH // 2
    rb = min(CONV_ROWS, T)

    def col_body(c, carry):
        c0 = pl.multiple_of(c * LANES, LANES)
        w = dw_ref[:, pl.ds(c0, LANES)]
        for r0 in range(0, T, rb):
            acc = jnp.zeros((rb, LANES), f32)
            span = rb + 2 * hb
            a = vbuf[pl.ds(r0, span), pl.ds(c0, LANES)]
            for rho in range(SUBLANES):
                s = a if rho == 0 else pltpu.roll(a, span - rho, axis=0)
                for k in range(CONV_WIDTH):
                    if (k + off) % SUBLANES == rho:
                        q = k + off - rho
                        acc = acc + w[k:k + 1, :] * s[q:q + rb]
            cbuf[pl.ds(r0, rb), pl.ds(c0, LANES)] = acc
        return carry

    lax.fori_loop(0, d // LANES, col_body, 0)
    cv = cbuf[...] + dwb_ref[...]
    xc = cv - jnp.mean(cv, axis=-1, keepdims=True)
    var = jnp.mean(xc * xc, axis=-1, keepdims=True)
    ln = xc * lax.rsqrt(var + EPS) * lng_ref[...] + lnb_ref[...]
    act = (ln * jax.nn.sigmoid(ln)).astype(bf16)
    y = jnp.dot(act, w2_ref[...], preferred_element_type=f32) + b2_ref[...]
    o_ref[...] = x_ref[...] + gate_ref[0] * y


def _cconv_layer(v, dw, dwb, lng, lnb, w2, b2, x, gate, *, L, name):
    n_rows, d = x.shape
    T = _tile(L, CONV_TILE)
    tps = L // T
    nseg = gate.shape[0]
    seg_of = (lambda i: i // tps) if nseg > 1 else (lambda i: 0)
    prev, nxt = _halo_specs(T, CONV_HALO, n_rows, d)
    tile = pl.BlockSpec((T, d), lambda i: (i, 0))
    return pl.pallas_call(
        functools.partial(_cconv_kernel, T=T, tps=tps),
        grid=(n_rows // T,),
        in_specs=[tile, prev, nxt, pl.BlockSpec((CONV_WIDTH, d), lambda i: (0, 0)), _row_spec(d), _row_spec(d),
                  _row_spec(d), pl.BlockSpec((d, d), lambda i: (0, 0)), _row_spec(d), tile, _seg_spec(d, seg_of)],
        out_specs=tile,
        out_shape=jax.ShapeDtypeStruct((n_rows, d), f32),
        scratch_shapes=[pltpu.VMEM((T + 2 * CONV_HALO, d), f32), pltpu.VMEM((T, d), f32)],
        compiler_params=_params("parallel"),
        name=name,
    )(v, v, v, dw, dwb, lng, lnb, w2, b2, x, gate)


def _lru_scan_kernel(xw_ref, xp_ref, xn_ref, cw_ref, cb_ref, wg_ref, ba_ref, bi_ref, lam_ref, h0_ref,
                     s_ref, ht_ref, xbuf, abuf, ubuf, st_ref, *, T, tps, reverse):
    j = pl.program_id(1)
    it = (tps - 1 - j) if reverse else j
    hb = SUBLANES
    d = xw_ref.shape[1]
    blk = d // LRU_HEADS

    @pl.when(j == 0)
    def _():
        st_ref[...] = h0_ref[0]

    xbuf[0:hb] = jnp.where(it == 0, 0.0, xp_ref[...])
    xbuf[hb:hb + T] = xw_ref[...]
    xbuf[hb + T:2 * hb + T] = jnp.where(it == tps - 1, 0.0, xn_ref[...])
    pad_lo = LRU_CONV_WIDTH // 2
    softplus_neg_lam = jax.nn.softplus(-lam_ref[...])
    for hd in range(LRU_HEADS):
        c0 = hd * blk
        xc = cb_ref[:, c0:c0 + blk] + jnp.zeros((T, blk), f32)
        xa = xbuf[:, c0:c0 + blk]
        for k in range(LRU_CONV_WIDTH):
            rho = (hb + k - pad_lo) % SUBLANES
            q = hb + k - pad_lo - rho
            s = xa if rho == 0 else pltpu.roll(xa, T + 2 * hb - rho, axis=0)
            xc = xc + cw_ref[k:k + 1, c0:c0 + blk] * s[q:q + T]
        z = jnp.dot(xc.astype(bf16), wg_ref[hd], preferred_element_type=f32)
        r = jax.nn.sigmoid(z[:, :blk] + ba_ref[:, c0:c0 + blk])
        ig = jax.nn.sigmoid(z[:, blk:] + bi_ref[:, c0:c0 + blk])
        log_a = -LRU_C * r * softplus_neg_lam[:, c0:c0 + blk]
        a = jnp.exp(log_a)
        one_minus_a2 = -jnp.tanh(log_a) * (a * a + 1.0)
        abuf[:, c0:c0 + blk] = a
        ubuf[:, c0:c0 + blk] = jnp.sqrt(one_minus_a2) * ig * xc

    n_grp = T // SUBLANES

    def grp(gi, h):
        g = (n_grp - 1 - gi) if reverse else gi
        r0 = pl.multiple_of(g * SUBLANES, SUBLANES)
        a = abuf[pl.ds(r0, SUBLANES), :]
        u = ubuf[pl.ds(r0, SUBLANES), :]
        rows = [None] * SUBLANES
        for r in (range(SUBLANES - 1, -1, -1) if reverse else range(SUBLANES)):
            h = a[r:r + 1, :] * h + u[r:r + 1, :]
            rows[r] = h
        s_ref[pl.ds(r0, SUBLANES), :] = jnp.concatenate(rows, axis=0)
        return h

    h_fin = lax.fori_loop(0, n_grp, grp, st_ref[...])
    st_ref[...] = h_fin
    ht_ref[0] = h_fin


def _lru_scan(xw, cw, cb, wg, ba, bi, lam, h0, *, L, reverse, name):
    n_rows, d = xw.shape
    nb = n_rows // L
    T = _tile(L, SEQ_TILE)
    tps = L // T

    def tile_of(b, j):
        return b * tps + ((tps - 1 - j) if reverse else j)

    r = T // SUBLANES
    last = n_rows // SUBLANES - 1
    tile = pl.BlockSpec((T, d), lambda b, j: (tile_of(b, j), 0))
    prev = pl.BlockSpec((SUBLANES, d), lambda b, j: (jnp.maximum(tile_of(b, j) * r - 1, 0), 0))
    nxt = pl.BlockSpec((SUBLANES, d), lambda b, j: (jnp.minimum((tile_of(b, j) + 1) * r, last), 0))
    row = pl.BlockSpec((1, d), lambda b, j: (0, 0))
    state = pl.BlockSpec((1, 1, d), lambda b, j: (b, 0, 0))
    return pl.pallas_call(
        functools.partial(_lru_scan_kernel, T=T, tps=tps, reverse=reverse),
        grid=(nb, tps),
        in_specs=[tile, prev, nxt, pl.BlockSpec((LRU_CONV_WIDTH, d), lambda b, j: (0, 0)), row,
                  pl.BlockSpec(wg.shape, lambda b, j: (0, 0, 0)), row, row, row, state],
        out_specs=[tile, state],
        out_shape=[jax.ShapeDtypeStruct((n_rows, d), f32), jax.ShapeDtypeStruct((nb, 1, d), f32)],
        scratch_shapes=[pltpu.VMEM((T + 2 * SUBLANES, d), f32), pltpu.VMEM((T, d), f32), pltpu.VMEM((T, d), f32),
                        pltpu.VMEM((1, d), f32)],
        compiler_params=_params("parallel", "arbitrary"),
        name=name,
    )(xw, xw, xw, cw, cb, wg, ba, bi, lam, h0)


def _lru_out_kernel(sf_ref, sb_ref, gy_ref, w_ref, x_ref, gate_ref, o_ref):
    y = ((sf_ref[...] + sb_ref[...]) * gy_ref[...]).astype(bf16)
    o_ref[...] = x_ref[...] + gate_ref[0] * jnp.dot(y, w_ref[...], preferred_element_type=f32)


def _lru_out(sf, sb, gy, w_out, x, gate, *, L, name):
    n_rows, d = x.shape
    T = _tile(L, CONV_TILE)
    tps = L // T
    nseg = gate.shape[0]
    seg_of = (lambda i: i // tps) if nseg > 1 else (lambda i: 0)
    tile = pl.BlockSpec((T, d), lambda i: (i, 0))
    return pl.pallas_call(
        _lru_out_kernel,
        grid=(n_rows // T,),
        in_specs=[tile, tile, tile, pl.BlockSpec((d, d), lambda i: (0, 0)), tile, _seg_spec(d, seg_of)],
        out_specs=tile,
        out_shape=jax.ShapeDtypeStruct((n_rows, d), f32),
        compiler_params=_params("parallel"),
        name=name,
    )(sf, sb, gy, w_out, x, gate)


def _ffn_kernel(*refs, T, W, halo, tps, fc, final):
    x_ref = refs[0]
    refs = refs[1:]
    if halo:
        xt_ref, xb_ref = refs[:2]
        refs = refs[2:]
    g_ref, sc_ref, sh_ref, gate_ref, wg_ref, wv_ref, cwg_ref, cwv_ref, cbg_ref, cbv_ref, wd_ref = refs[:11]
    refs = refs[11:]
    if final:
        fg_ref = refs[0]
        refs = refs[1:]
    o_ref, h_ref, ug_ref, uv_ref, act_ref = refs
    i = pl.program_id(0)
    k = pl.program_id(1)
    it = i % tps
    n_rows = T + 2 * halo
    eb = FFN_EDGE_ROWS if T % FFN_EDGE_ROWS == 0 else T

    @pl.when(k == 0)
    def _():
        g = g_ref[...]
        sc = sc_ref[0]
        sh = sh_ref[0]
        for r0 in range(0, T, eb):
            h_ref[halo + r0:halo + r0 + eb] = _rms_mod(x_ref[r0:r0 + eb], g, sc, sh).astype(bf16)
        if halo:
            top = _rms_mod(xt_ref[...], g, sc, sh)
            bot = _rms_mod(xb_ref[...], g, sc, sh)
            h_ref[0:halo] = jnp.where(it == 0, 0.0, top).astype(bf16)
            h_ref[halo + T:n_rows] = jnp.where(it == tps - 1, 0.0, bot).astype(bf16)
        o_ref[...] = jnp.zeros_like(o_ref)

    wpos = lax.broadcasted_iota(jnp.int32, (SUBLANES, 1), 0)

    def shift_rows(p, up):
        rolled = pltpu.roll(p, W - 1 if up else 1, axis=0)
        e0 = W - SUBLANES if up else 0
        edge = jnp.where(wpos == (SUBLANES - 1 if up else 0), 0.0, rolled[e0:e0 + SUBLANES])
        parts = [rolled[:e0], edge] if up else [edge, rolled[SUBLANES:]]
        return jnp.concatenate(parts, axis=0)

    def conv_row(u_ref, cw_ref, cb_ref, r, lanes):
        p = [None] * 3
        for dr in ((0, 1, 2) if halo else (1,)):
            o = (r + dr) * W if halo else r * W
            blk = u_ref[o:o + W, lanes]
            for dw in range(3):
                t = cw_ref[3 * dr + dw:3 * dr + dw + 1, lanes] * blk
                p[dw] = t if p[dw] is None else p[dw] + t
        return p[1] + shift_rows(p[0], False) + shift_rows(p[2], True) + cb_ref[:, lanes]

    n_out = T // W
    hr = n_out + (2 if halo else 0)
    n_grp = next(gn for gn in (FFN_ROW_GROUPS, 2, 1) if hr % gn == 0)
    gsz = hr // n_grp
    for m in range(n_grp):
        rows = slice(m * gsz * W, (m + 1) * gsz * W)
        ug_ref[rows] = jnp.dot(h_ref[rows], wg_ref[...], preferred_element_type=f32)
        uv_ref[rows] = jnp.dot(h_ref[rows], wv_ref[...], preferred_element_type=f32)
    done = 0
    for m in range(n_grp):
        last = n_out if m == n_grp - 1 else max(done, min(n_out, (m + 1) * gsz - (2 if halo else 0)))
        if last == done:
            continue
        for c in range(0, fc, LANES):
            lanes = slice(c, c + LANES)
            for r in range(done, last):
                gv = conv_row(ug_ref, cwg_ref, cbg_ref, r, lanes)
                vv = conv_row(uv_ref, cwv_ref, cbv_ref, r, lanes)
                act_ref[r * W:(r + 1) * W, lanes] = (jax.nn.gelu(gv) * vv).astype(bf16)
        orows = slice(done * W, last * W)
        o_ref[orows] += jnp.dot(act_ref[orows], wd_ref[...], preferred_element_type=f32)
        done = last

    @pl.when(k == pl.num_programs(1) - 1)
    def _():
        for r0 in range(0, T, eb):
            y = x_ref[r0:r0 + eb] + gate_ref[0] * o_ref[r0:r0 + eb]
            if final:
                y = y * lax.rsqrt(jnp.mean(y * y, axis=-1, keepdims=True) + EPS) * fg_ref[...]
            o_ref[r0:r0 + eb] = y


def _ffn_layer(x, g, sc, sh, gate, w_up, cw, cb, w_down, final_g, *, layer, L, two_d, name):
    n_rows, d = x.shape
    F = w_down.shape[1]
    fc = min(FFN_CHUNK, F)
    nseg = sc.shape[0]
    if two_d:
        W = GRID_W
        T = min(L, FFN_ROWS * W)
        halo = W
        tps = L // T
        assert L % T == 0
    else:
        assert nseg == 1
        W = L
        T = n_rows
        halo = 0
        tps = 1
    assert F % fc == 0
    nk = F // fc
    seg_of = (lambda i: i // tps) if nseg > 1 else (lambda i: 0)
    tile = pl.BlockSpec((T, d), lambda i, k: (i, 0), pipeline_mode=pl.Buffered(1))
    in_specs = [tile]
    args = [x]
    if halo:
        in_specs += list(_halo_specs(T, halo, n_rows, d))
        args += [x, x]
    in_specs += [_row_spec(d), _seg_spec(d, seg_of), _seg_spec(d, seg_of), _seg_spec(d, seg_of),
                 pl.BlockSpec((None, d, fc), lambda i, k: (layer, 0, k)),
                 pl.BlockSpec((None, d, fc), lambda i, k: (layer, 0, k + nk)),
                 pl.BlockSpec((None, 9, fc), lambda i, k: (layer, 0, k)),
                 pl.BlockSpec((None, 9, fc), lambda i, k: (layer, 0, k + nk)),
                 pl.BlockSpec((None, 1, fc), lambda i, k: (layer, 0, k)),
                 pl.BlockSpec((None, 1, fc), lambda i, k: (layer, 0, k + nk)),
                 pl.BlockSpec((None, fc, d), lambda i, k: (layer, k, 0))]
    args += [g, sc, sh, gate, w_up, w_up, cw, cw, cb, cb, w_down]
    final = final_g is not None
    if final:
        in_specs.append(_row_spec(d))
        args.append(final_g)
    return pl.pallas_call(
        functools.partial(_ffn_kernel, T=T, W=W, halo=halo, tps=tps, fc=fc, final=final),
        grid=(n_rows // T, nk),
        in_specs=in_specs,
        out_specs=tile,
        out_shape=jax.ShapeDtypeStruct((n_rows, d), f32),
        scratch_shapes=[pltpu.VMEM((T + 2 * halo, d), bf16)] + [pltpu.VMEM((T + 2 * halo, fc), f32)] * 2
                       + [pltpu.VMEM((T, fc), bf16)],
        compiler_params=_params("parallel", "arbitrary"),
        name=name,
    )(*args)


def kernel(x, c, ctx, c_ctx, norm1_g, norm2_g, mod_w, mod_b, pool_w, pool_scale, cv_w1, cv_b1, cv_dw, cv_dw_b, cv_ln_g, cv_ln_b, cv_w2, cv_b2, lru_w_x, lru_w_y, lru_conv_w, lru_conv_b, lru_wa, lru_ba, lru_wi, lru_bi, lru_lambda, lru_w_out, ffn_w_up, ffn_conv_w, ffn_conv_b, ffn_w_down, final_g):
    B, L, D = x.shape
    Lc = ctx.shape[1]
    depth = norm1_g.shape[0]
    F2 = ffn_w_up.shape[2]

    c8 = jnp.concatenate([c, c_ctx[None, :], jnp.zeros((SUBLANES - B - 1, D), f32)], axis=0)
    mod = _modulation(c8, mod_w, mod_b).reshape(depth, SUBLANES, N_MOD, 1, D)

    xl = x.reshape(B * L, D)
    xc = ctx.reshape(B * Lc, D)
    row = lambda v: v.reshape(1, -1)
    w_up = ffn_w_up.astype(bf16)
    w_dn = ffn_w_down.astype(bf16)
    cw = ffn_conv_w.reshape(depth, 9, F2)
    cb = ffn_conv_b.reshape(depth, 1, F2)

    for i in range(depth):
        kind = i % N_MIXERS
        j = i // N_MIXERS
        ctx_later = any(k % N_MIXERS == 2 for k in range(i + 1, depth))
        ctx_mix = ctx_later or kind == 2
        m_lat = [mod[i, 0:B, q] for q in range(N_MOD)]
        m_ctx = [mod[i, B:B + 1, q] for q in range(N_MOD)]
        streams = [("lat", xl, L, m_lat, True)]
        if ctx_mix:
            streams.append(("ctx", xc, Lc, m_ctx, ctx_later))
        g1n = row(norm1_g[i])
        g2n = row(norm2_g[i])
        new = {}
        if kind == 0:
            pw = pool_w[j].astype(bf16)
            for nm, xs, Ls, m, full in streams:
                if full:
                    new[nm] = _pool_layer(xs, g1n, m[1], m[0], m[2], pw, row(pool_scale[j]), L=Ls,
                                          name=f"pool{i}_{nm}")
        elif kind == 1:
            w1 = cv_w1[j].astype(bf16)
            w2 = cv_w2[j].astype(bf16)
            b1 = row(cv_b1[j])
            for nm, xs, Ls, m, full in streams:
                if full:
                    v = _prenorm_dual(xs, g1n, m[1], m[0], w1, w1, b1, b1, mode="glu", L=Ls,
                                      wb_off=D // min(PROJ_COLS, D), name=f"cv_glu{i}_{nm}")
                    new[nm] = _cconv_layer(v, cv_dw[j], row(cv_dw_b[j]), row(cv_ln_g[j]), row(cv_ln_b[j]), w2,
                                           row(cv_b2[j]), xs, m[2], L=Ls, name=f"cv_tail{i}_{nm}")
        else:
            wx = lru_w_x[j].astype(bf16)
            wy = lru_w_y[j].astype(bf16)
            wo = lru_w_out[j].astype(bf16)
            wg = jnp.concatenate([lru_wa[j], lru_wi[j]], axis=-1).astype(bf16)
            h0 = [jnp.zeros((B, 1, D), f32)] * 2
            for nm, xs, Ls, m, full in reversed(streams):
                xw, gy = _prenorm_dual(xs, g1n, m[1], m[0], wx, wy, None, None, mode="lru", L=Ls,
                                       wb_off=0, name=f"lru_in{i}_{nm}")
                s, ht = [], []
                for dr in range(2):
                    sd, hd = _lru_scan(xw, lru_conv_w[j], row(lru_conv_b[j]), wg[dr], row(lru_ba[j, dr]),
                                       row(lru_bi[j, dr]), row(lru_lambda[j, dr]), h0[dr], L=Ls,
                                       reverse=bool(dr), name=f"lru_scan{i}_{nm}{dr}")
                    s.append(sd)
                    ht.append(hd)
                h0 = ht
                if full:
                    new[nm] = _lru_out(s[0], s[1], gy, wo, xs, m[2], L=Ls, name=f"lru_out{i}_{nm}")
        fin = row(final_g) if i == depth - 1 else None
        xl = _ffn_layer(new["lat"], g2n, m_lat[4], m_lat[3], m_lat[5], w_up, cw, cb, w_dn, fin, layer=i, L=L,
                        two_d=True, name=f"ffn{i}_lat")
        if ctx_later:
            xc = _ffn_layer(new["ctx"], g2n, m_ctx[4], m_ctx[3], m_ctx[5], w_up, cw, cb, w_dn, None, layer=i,
                            L=Lc, two_d=False, name=f"ffn{i}_ctx")
    return xl.reshape(B, L, D)
```

```python
import functools

import jax
import jax.numpy as jnp
from jax import lax
from jax.experimental import pallas as pl
from jax.experimental.pallas import tpu as pltpu

GRID_W = 64
N_MIXERS = 3
POOL_WINDOWS = (2, 4, 8, 16)
CONV_WIDTH = 31
LRU_HEADS = 16
LRU_CONV_WIDTH = 4
LRU_C = 8.0
N_MOD = 6
EPS = 1e-6

LANES = 128
SUBLANES = 8
VMEM_LIMIT_BYTES = 56 * 1024 * 1024
SEQ_TILE = 512
FFN_ROWS = 16
FFN_CHUNK = 512
FFN_EDGE_ROWS = 64
FFN_GROUP_ROWS = (7, 6, 5)
CONV_HALO = 16
CONV_ROWS = 64
CONV_TILE = 256
PROJ_COLS = 512
MOD_COLS = 1024

bf16 = jnp.bfloat16
f32 = jnp.float32


def _params(*sem):
    return pltpu.CompilerParams(dimension_semantics=sem, vmem_limit_bytes=VMEM_LIMIT_BYTES)


def _tile(L, pref):
    t = min(L, pref)
    assert L % t == 0 and t % SUBLANES == 0
    return t


def _rms_mod(x, g, sc, sh):
    y = x * lax.rsqrt(jnp.mean(x * x, axis=-1, keepdims=True) + EPS)
    return (y * g) * (1.0 + sc) + sh


def _row_spec(n):
    return pl.BlockSpec((1, n), lambda *_: (0, 0))


def _seg_spec(d, seg_of):
    return pl.BlockSpec((1, 1, d), lambda i, *_: (seg_of(i), 0, 0))


def _halo_specs(T, hb, n_rows, d):
    r = T // hb
    last = n_rows // hb - 1
    prev = pl.BlockSpec((hb, d), lambda i, *_: (jnp.maximum(i * r - 1, 0), 0))
    nxt = pl.BlockSpec((hb, d), lambda i, *_: (jnp.minimum((i + 1) * r, last), 0))
    return prev, nxt


def _mod_kernel(c_ref, w_ref, b_ref, o_ref):
    c = c_ref[...]
    s = (c * jax.nn.sigmoid(c)).astype(bf16)
    o_ref[...] = jnp.dot(s, w_ref[...].astype(bf16), preferred_element_type=f32) + b_ref[...]


def _modulation(c8, mod_w, mod_b):
    depth, d, n = mod_w.shape
    tn = min(MOD_COLS, n)
    return pl.pallas_call(
        _mod_kernel,
        grid=(depth, n // tn),
        in_specs=[pl.BlockSpec((SUBLANES, d), lambda l, j: (0, 0)),
                  pl.BlockSpec((None, d, tn), lambda l, j: (l, 0, j)),
                  pl.BlockSpec((None, 1, tn), lambda l, j: (l, 0, j))],
        out_specs=pl.BlockSpec((None, SUBLANES, tn), lambda l, j: (l, 0, j)),
        out_shape=jax.ShapeDtypeStruct((depth, SUBLANES, n), f32),
        compiler_params=_params("parallel", "parallel"),
        name="modulation",
    )(c8, mod_w, mod_b.reshape(depth, 1, n))


def _prenorm_dual_kernel(*refs, mode, has_bias):
    x_ref, g_ref, sc_ref, sh_ref, wa_ref, wb_ref = refs[:6]
    refs = refs[6:]
    if has_bias:
        ba_ref, bb_ref = refs[:2]
        refs = refs[2:]
    h_ref = refs[-1]
    outs = refs[:-1]

    @pl.when(pl.program_id(1) == 0)
    def _():
        h_ref[...] = _rms_mod(x_ref[...], g_ref[...], sc_ref[0], sh_ref[0]).astype(bf16)

    h = h_ref[...]
    a = jnp.dot(h, wa_ref[...], preferred_element_type=f32)
    b = jnp.dot(h, wb_ref[...], preferred_element_type=f32)
    if has_bias:
        a = a + ba_ref[...]
        b = b + bb_ref[...]
    if mode == "glu":
        outs[0][...] = a * jax.nn.sigmoid(b)
    else:
        outs[0][...] = a
        outs[1][...] = jax.nn.gelu(b)


def _prenorm_dual(x, g, sc, sh, wa, wb, ba, bb, *, mode, L, wb_off, name):
    n_rows, d = x.shape
    T = _tile(L, SEQ_TILE)
    tps = L // T
    nseg = sc.shape[0]
    seg_of = (lambda i: i // tps) if nseg > 1 else (lambda i: 0)
    n_out = d
    tn = min(PROJ_COLS, n_out)
    has_bias = ba is not None
    in_specs = [pl.BlockSpec((T, d), lambda i, j: (i, 0)), _row_spec(d), _seg_spec(d, seg_of), _seg_spec(d, seg_of),
                pl.BlockSpec((d, tn), lambda i, j: (0, j)),
                pl.BlockSpec((d, tn), lambda i, j: (0, j + wb_off))]
    args = [x, g, sc, sh, wa, wb]
    if has_bias:
        in_specs += [pl.BlockSpec((1, tn), lambda i, j: (0, j)), pl.BlockSpec((1, tn), lambda i, j: (0, j + wb_off))]
        args += [ba, bb]
    o_spec = pl.BlockSpec((T, tn), lambda i, j: (i, j))
    o_shape = jax.ShapeDtypeStruct((n_rows, n_out), f32)
    n_o = 1 if mode == "glu" else 2
    res = pl.pallas_call(
        functools.partial(_prenorm_dual_kernel, mode=mode, has_bias=has_bias),
        grid=(n_rows // T, n_out // tn),
        in_specs=in_specs,
        out_specs=[o_spec] * n_o,
        out_shape=[o_shape] * n_o,
        scratch_shapes=[pltpu.VMEM((T, d), bf16)],
        compiler_params=_params("parallel", "arbitrary"),
        name=name,
    )(*args)
    return res[0] if n_o == 1 else res


def _pool_kernel(x_ref, xp_ref, xn_ref, g_ref, sc_ref, sh_ref, gate_ref, w_ref, scale_ref, o_ref, hbuf,
                 *, T, L, tps, G):
    it = pl.program_id(0) % tps
    g = g_ref[...]
    sc = sc_ref[0]
    sh = sh_ref[0]
    gate = gate_ref[0]
    hb = SUBLANES
    x = x_ref[...]
    h = _rms_mod(x, g, sc, sh)
    hbuf[0:hb] = jnp.where(it == 0, 0.0, _rms_mod(xp_ref[...], g, sc, sh))
    hbuf[hb:hb + T] = h
    hbuf[hb + T:2 * hb + T] = jnp.where(it == tps - 1, 0.0, _rms_mod(xn_ref[...], g, sc, sh))
    tpos = it * T + lax.broadcasted_iota(jnp.int32, (T, 1), 0)
    for gi, win in enumerate(POOL_WINDOWS):
        half = win // 2
        c0 = gi * G
        n = T + 2 * hb
        shift = lambda a, d: pltpu.roll(a, (-d) % n, axis=0)
        w = hbuf[:, c0:c0 + G]
        w = shift(w, -1) + w
        span = 2
        while span < win:
            w = shift(w, -(span // 2)) + shift(w, span // 2)
            span *= 2
        s = w[hb:hb + T]
        cnt = (jnp.minimum(tpos + half, L) - jnp.maximum(tpos - half, 0)).astype(f32)
        dlt = (s / cnt - h[:, c0:c0 + G]).astype(bf16)
        y = jnp.dot(dlt, w_ref[gi], preferred_element_type=f32) * scale_ref[:, c0:c0 + G]
        o_ref[:, c0:c0 + G] = x[:, c0:c0 + G] + gate[:, c0:c0 + G] * y


def _pool_layer(x, g, sc, sh, gate, w, scale, *, L, name):
    n_rows, d = x.shape
    T = _tile(L, SEQ_TILE)
    tps = L // T
    nseg = sc.shape[0]
    seg_of = (lambda i: i // tps) if nseg > 1 else (lambda i: 0)
    ng, G, _ = w.shape
    assert max(POOL_WINDOWS) // 2 <= SUBLANES and all(w >= 2 and w & (w - 1) == 0 for w in POOL_WINDOWS)
    prev, nxt = _halo_specs(T, SUBLANES, n_rows, d)
    return pl.pallas_call(
        functools.partial(_pool_kernel, T=T, L=L, tps=tps, G=G),
        grid=(n_rows // T,),
        in_specs=[pl.BlockSpec((T, d), lambda i: (i, 0)), prev, nxt, _row_spec(d),
                  _seg_spec(d, seg_of), _seg_spec(d, seg_of), _seg_spec(d, seg_of),
                  pl.BlockSpec((ng, G, G), lambda i: (0, 0, 0)), _row_spec(d)],
        out_specs=pl.BlockSpec((T, d), lambda i: (i, 0)),
        out_shape=jax.ShapeDtypeStruct((n_rows, d), f32),
        scratch_shapes=[pltpu.VMEM((T + 2 * SUBLANES, d), f32)],
        compiler_params=_params("parallel"),
        name=name,
    )(x, x, x, g, sc, sh, gate, w, scale)


def _cconv_kernel(v_ref, vp_ref, vn_ref, dw_ref, dwb_ref, lng_ref, lnb_ref, w2_ref, b2_ref, x_ref, gate_ref,
                  o_ref, vbuf, cbuf, *, T, tps):
    it = pl.program_id(0) % tps
    hb = CONV_HALO
    d = v_ref.shape[1]
    vbuf[0:hb] = jnp.where(it == 0, 0.0, vp_ref[...])
    vbuf[hb:hb + T] = v_ref[...]
    vbuf[hb + T:2 * hb + T] = jnp.where(it == tps - 1, 0.0, vn_ref[...])
    off = hb - CONV_WIDTH // 2
    rb = min(CONV_ROWS, T)

    def col_body(c, carry):
        c0 = pl.multiple_of(c * LANES, LANES)
        w = dw_ref[:, pl.ds(c0, LANES)]
        for r0 in range(0, T, rb):
            acc = jnp.zeros((rb, LANES), f32)
            span = rb + 2 * hb
            a = vbuf[pl.ds(r0, span), pl.ds(c0, LANES)]
            for rho in range(SUBLANES):
                s = a if rho == 0 else pltpu.roll(a, span - rho, axis=0)
                for k in range(CONV_WIDTH):
                    if (k + off) % SUBLANES == rho:
                        q = k + off - rho
                        acc = acc + w[k:k + 1, :] * s[q:q + rb]
            cbuf[pl.ds(r0, rb), pl.ds(c0, LANES)] = acc
        return carry

    lax.fori_loop(0, d // LANES, col_body, 0)
    cv = cbuf[...] + dwb_ref[...]
    xc = cv - jnp.mean(cv, axis=-1, keepdims=True)
    var = jnp.mean(xc * xc, axis=-1, keepdims=True)
    ln = xc * lax.rsqrt(var + EPS) * lng_ref[...] + lnb_ref[...]
    act = (ln * jax.nn.sigmoid(ln)).astype(bf16)
    y = jnp.dot(act, w2_ref[...], preferred_element_type=f32) + b2_ref[...]
    o_ref[...] = x_ref[...] + gate_ref[0] * y


def _cconv_layer(v, dw, dwb, lng, lnb, w2, b2, x, gate, *, L, name):
    n_rows, d = x.shape
    T = _tile(L, CONV_TILE)
    tps = L // T
    nseg = gate.shape[0]
    seg_of = (lambda i: i // tps) if nseg > 1 else (lambda i: 0)
    prev, nxt = _halo_specs(T, CONV_HALO, n_rows, d)
    tile = pl.BlockSpec((T, d), lambda i: (i, 0))
    return pl.pallas_call(
        functools.partial(_cconv_kernel, T=T, tps=tps),
        grid=(n_rows // T,),
        in_specs=[tile, prev, nxt, pl.BlockSpec((CONV_WIDTH, d), lambda i: (0, 0)), _row_spec(d), _row_spec(d),
                  _row_spec(d), pl.BlockSpec((d, d), lambda i: (0, 0)), _row_spec(d), tile, _seg_spec(d, seg_of)],
        out_specs=tile,
        out_shape=jax.ShapeDtypeStruct((n_rows, d), f32),
        scratch_shapes=[pltpu.VMEM((T + 2 * CONV_HALO, d), f32), pltpu.VMEM((T, d), f32)],
        compiler_params=_params("parallel"),
        name=name,
    )(v, v, v, dw, dwb, lng, lnb, w2, b2, x, gate)


def _lru_scan_kernel(xw_ref, xp_ref, xn_ref, cw_ref, cb_ref, wg_ref, ba_ref, bi_ref, lam_ref, h0_ref,
                     s_ref, ht_ref, xbuf, abuf, ubuf, st_ref, *, T, tps, reverse):
    j = pl.program_id(1)
    it = (tps - 1 - j) if reverse else j
    hb = SUBLANES
    d = xw_ref.shape[1]
    blk = d // LRU_HEADS

    @pl.when(j == 0)
    def _():
        st_ref[...] = h0_ref[0]

    xbuf[0:hb] = jnp.where(it == 0, 0.0, xp_ref[...])
    xbuf[hb:hb + T] = xw_ref[...]
    xbuf[hb + T:2 * hb + T] = jnp.where(it == tps - 1, 0.0, xn_ref[...])
    pad_lo = LRU_CONV_WIDTH // 2
    softplus_neg_lam = jax.nn.softplus(-lam_ref[...])
    for hd in range(LRU_HEADS):
        c0 = hd * blk
        xc = cb_ref[:, c0:c0 + blk] + jnp.zeros((T, blk), f32)
        xa = xbuf[:, c0:c0 + blk]
        for k in range(LRU_CONV_WIDTH):
            rho = (hb + k - pad_lo) % SUBLANES
            q = hb + k - pad_lo - rho
            s = xa if rho == 0 else pltpu.roll(xa, T + 2 * hb - rho, axis=0)
            xc = xc + cw_ref[k:k + 1, c0:c0 + blk] * s[q:q + T]
        z = jnp.dot(xc.astype(bf16), wg_ref[hd], preferred_element_type=f32)
        r = jax.nn.sigmoid(z[:, :blk] + ba_ref[:, c0:c0 + blk])
        ig = jax.nn.sigmoid(z[:, blk:] + bi_ref[:, c0:c0 + blk])
        log_a = -LRU_C * r * softplus_neg_lam[:, c0:c0 + blk]
        a = jnp.exp(log_a)
        one_minus_a2 = -jnp.tanh(log_a) * (a * a + 1.0)
        abuf[:, c0:c0 + blk] = a
        ubuf[:, c0:c0 + blk] = jnp.sqrt(one_minus_a2) * ig * xc

    n_grp = T // SUBLANES

    def grp(gi, h):
        g = (n_grp - 1 - gi) if reverse else gi
        r0 = pl.multiple_of(g * SUBLANES, SUBLANES)
        a = abuf[pl.ds(r0, SUBLANES), :]
        u = ubuf[pl.ds(r0, SUBLANES), :]
        rows = [None] * SUBLANES
        for r in (range(SUBLANES - 1, -1, -1) if reverse else range(SUBLANES)):
            h = a[r:r + 1, :] * h + u[r:r + 1, :]
            rows[r] = h
        s_ref[pl.ds(r0, SUBLANES), :] = jnp.concatenate(rows, axis=0)
        return h

    h_fin = lax.fori_loop(0, n_grp, grp, st_ref[...])
    st_ref[...] = h_fin
    ht_ref[0] = h_fin


def _lru_scan(xw, cw, cb, wg, ba, bi, lam, h0, *, L, reverse, name):
    n_rows, d = xw.shape
    nb = n_rows // L
    T = _tile(L, SEQ_TILE)
    tps = L // T

    def tile_of(b, j):
        return b * tps + ((tps - 1 - j) if reverse else j)

    r = T // SUBLANES
    last = n_rows // SUBLANES - 1
    tile = pl.BlockSpec((T, d), lambda b, j: (tile_of(b, j), 0))
    prev = pl.BlockSpec((SUBLANES, d), lambda b, j: (jnp.maximum(tile_of(b, j) * r - 1, 0), 0))
    nxt = pl.BlockSpec((SUBLANES, d), lambda b, j: (jnp.minimum((tile_of(b, j) + 1) * r, last), 0))
    row = pl.BlockSpec((1, d), lambda b, j: (0, 0))
    state = pl.BlockSpec((1, 1, d), lambda b, j: (b, 0, 0))
    return pl.pallas_call(
        functools.partial(_lru_scan_kernel, T=T, tps=tps, reverse=reverse),
        grid=(nb, tps),
        in_specs=[tile, prev, nxt, pl.BlockSpec((LRU_CONV_WIDTH, d), lambda b, j: (0, 0)), row,
                  pl.BlockSpec(wg.shape, lambda b, j: (0, 0, 0)), row, row, row, state],
        out_specs=[tile, state],
        out_shape=[jax.ShapeDtypeStruct((n_rows, d), f32), jax.ShapeDtypeStruct((nb, 1, d), f32)],
        scratch_shapes=[pltpu.VMEM((T + 2 * SUBLANES, d), f32), pltpu.VMEM((T, d), f32), pltpu.VMEM((T, d), f32),
                        pltpu.VMEM((1, d), f32)],
        compiler_params=_params("parallel", "arbitrary"),
        name=name,
    )(xw, xw, xw, cw, cb, wg, ba, bi, lam, h0)


def _lru_out_kernel(sf_ref, sb_ref, gy_ref, w_ref, x_ref, gate_ref, o_ref):
    y = ((sf_ref[...] + sb_ref[...]) * gy_ref[...]).astype(bf16)
    o_ref[...] = x_ref[...] + gate_ref[0] * jnp.dot(y, w_ref[...], preferred_element_type=f32)


def _lru_out(sf, sb, gy, w_out, x, gate, *, L, name):
    n_rows, d = x.shape
    T = _tile(L, CONV_TILE)
    tps = L // T
    nseg = gate.shape[0]
    seg_of = (lambda i: i // tps) if nseg > 1 else (lambda i: 0)
    tile = pl.BlockSpec((T, d), lambda i: (i, 0))
    return pl.pallas_call(
        _lru_out_kernel,
        grid=(n_rows // T,),
        in_specs=[tile, tile, tile, pl.BlockSpec((d, d), lambda i: (0, 0)), tile, _seg_spec(d, seg_of)],
        out_specs=tile,
        out_shape=jax.ShapeDtypeStruct((n_rows, d), f32),
        compiler_params=_params("parallel"),
        name=name,
    )(sf, sb, gy, w_out, x, gate)


def _ffn_kernel(*refs, T, W, halo, tps, fc, final):
    x_ref = refs[0]
    refs = refs[1:]
    if halo:
        xt_ref, xb_ref = refs[:2]
        refs = refs[2:]
    g_ref, sc_ref, sh_ref, gate_ref, wg_ref, wv_ref, cwg_ref, cwv_ref, cbg_ref, cbv_ref, wd_ref = refs[:11]
    refs = refs[11:]
    if final:
        fg_ref = refs[0]
        refs = refs[1:]
    o_ref, h_ref, ug_ref, uv_ref, act_ref = refs
    i = pl.program_id(0)
    k = pl.program_id(1)
    it = i % tps
    n_rows = T + 2 * halo
    eb = FFN_EDGE_ROWS if T % FFN_EDGE_ROWS == 0 else T

    @pl.when(k == 0)
    def _():
        g = g_ref[...]
        sc = sc_ref[0]
        sh = sh_ref[0]
        for r0 in range(0, T, eb):
            h_ref[halo + r0:halo + r0 + eb] = _rms_mod(x_ref[r0:r0 + eb], g, sc, sh).astype(bf16)
        if halo:
            top = _rms_mod(xt_ref[...], g, sc, sh)
            bot = _rms_mod(xb_ref[...], g, sc, sh)
            h_ref[0:halo] = jnp.where(it == 0, 0.0, top).astype(bf16)
            h_ref[halo + T:n_rows] = jnp.where(it == tps - 1, 0.0, bot).astype(bf16)
        o_ref[...] = jnp.zeros_like(o_ref)

    wpos = lax.broadcasted_iota(jnp.int32, (SUBLANES, 1), 0)

    def shift_rows(p, up):
        rolled = pltpu.roll(p, W - 1 if up else 1, axis=0)
        e0 = W - SUBLANES if up else 0
        edge = jnp.where(wpos == (SUBLANES - 1 if up else 0), 0.0, rolled[e0:e0 + SUBLANES])
        parts = [rolled[:e0], edge] if up else [edge, rolled[SUBLANES:]]
        return jnp.concatenate(parts, axis=0)

    def conv_row(u_ref, cw_ref, cb_ref, r, lanes):
        p = [None] * 3
        for dr in ((0, 1, 2) if halo else (1,)):
            o = (r + dr) * W if halo else r * W
            blk = u_ref[o:o + W, lanes]
            for dw in range(3):
                t = cw_ref[3 * dr + dw:3 * dr + dw + 1, lanes] * blk
                p[dw] = t if p[dw] is None else p[dw] + t
        return p[1] + shift_rows(p[0], False) + shift_rows(p[2], True) + cb_ref[:, lanes]

    n_out = T // W
    hr = n_out + (2 if halo else 0)
    sizes = FFN_GROUP_ROWS if sum(FFN_GROUP_ROWS) == hr else (hr,)
    n_grp = len(sizes)
    ends = [sum(sizes[:m + 1]) for m in range(n_grp)]
    for m in range(n_grp):
        rows = slice((ends[m] - sizes[m]) * W, ends[m] * W)
        ug_ref[rows] = jnp.dot(h_ref[rows], wg_ref[...], preferred_element_type=f32)
        uv_ref[rows] = jnp.dot(h_ref[rows], wv_ref[...], preferred_element_type=f32)
    done = 0
    for m in range(n_grp):
        last = n_out if m == n_grp - 1 else max(done, min(n_out, ends[m] - (2 if halo else 0)))
        if last == done:
            continue
        for c in range(0, fc, LANES):
            lanes = slice(c, c + LANES)
            for r in range(done, last):
                gv = conv_row(ug_ref, cwg_ref, cbg_ref, r, lanes)
                vv = conv_row(uv_ref, cwv_ref, cbv_ref, r, lanes)
                act_ref[r * W:(r + 1) * W, lanes] = (jax.nn.gelu(gv) * vv).astype(bf16)
        orows = slice(done * W, last * W)
        o_ref[orows] += jnp.dot(act_ref[orows], wd_ref[...], preferred_element_type=f32)
        done = last

    @pl.when(k == pl.num_programs(1) - 1)
    def _():
        for r0 in range(0, T, eb):
            y = x_ref[r0:r0 + eb] + gate_ref[0] * o_ref[r0:r0 + eb]
            if final:
                y = y * lax.rsqrt(jnp.mean(y * y, axis=-1, keepdims=True) + EPS) * fg_ref[...]
            o_ref[r0:r0 + eb] = y


def _ffn_layer(x, g, sc, sh, gate, w_up, cw, cb, w_down, final_g, *, layer, L, two_d, name):
    n_rows, d = x.shape
    F = w_down.shape[1]
    fc = min(FFN_CHUNK, F)
    nseg = sc.shape[0]
    if two_d:
        W = GRID_W
        T = min(L, FFN_ROWS * W)
        halo = W
        tps = L // T
        assert L % T == 0
    else:
        assert nseg == 1
        W = L
        T = n_rows
        halo = 0
        tps = 1
    assert F % fc == 0
    nk = F // fc
    seg_of = (lambda i: i // tps) if nseg > 1 else (lambda i: 0)
    tile = pl.BlockSpec((T, d), lambda i, k: (i, 0), pipeline_mode=pl.Buffered(1))
    in_specs = [tile]
    args = [x]
    if halo:
        in_specs += list(_halo_specs(T, halo, n_rows, d))
        args += [x, x]
    in_specs += [_row_spec(d), _seg_spec(d, seg_of), _seg_spec(d, seg_of), _seg_spec(d, seg_of),
                 pl.BlockSpec((None, d, fc), lambda i, k: (layer, 0, k)),
                 pl.BlockSpec((None, d, fc), lambda i, k: (layer, 0, k + nk)),
                 pl.BlockSpec((None, 9, fc), lambda i, k: (layer, 0, k)),
                 pl.BlockSpec((None, 9, fc), lambda i, k: (layer, 0, k + nk)),
                 pl.BlockSpec((None, 1, fc), lambda i, k: (layer, 0, k)),
                 pl.BlockSpec((None, 1, fc), lambda i, k: (layer, 0, k + nk)),
                 pl.BlockSpec((None, fc, d), lambda i, k: (layer, k, 0))]
    args += [g, sc, sh, gate, w_up, w_up, cw, cw, cb, cb, w_down]
    final = final_g is not None
    if final:
        in_specs.append(_row_spec(d))
        args.append(final_g)
    return pl.pallas_call(
        functools.partial(_ffn_kernel, T=T, W=W, halo=halo, tps=tps, fc=fc, final=final),
        grid=(n_rows // T, nk),
        in_specs=in_specs,
        out_specs=tile,
        out_shape=jax.ShapeDtypeStruct((n_rows, d), f32),
        scratch_shapes=[pltpu.VMEM((T + 2 * halo, d), bf16)] + [pltpu.VMEM((T + 2 * halo, fc), f32)] * 2
                       + [pltpu.VMEM((T, fc), bf16)],
        compiler_params=_params("parallel", "arbitrary"),
        name=name,
    )(*args)


def kernel(x, c, ctx, c_ctx, norm1_g, norm2_g, mod_w, mod_b, pool_w, pool_scale, cv_w1, cv_b1, cv_dw, cv_dw_b, cv_ln_g, cv_ln_b, cv_w2, cv_b2, lru_w_x, lru_w_y, lru_conv_w, lru_conv_b, lru_wa, lru_ba, lru_wi, lru_bi, lru_lambda, lru_w_out, ffn_w_up, ffn_conv_w, ffn_conv_b, ffn_w_down, final_g):
    B, L, D = x.shape
    Lc = ctx.shape[1]
    depth = norm1_g.shape[0]
    F2 = ffn_w_up.shape[2]

    c8 = jnp.concatenate([c, c_ctx[None, :], jnp.zeros((SUBLANES - B - 1, D), f32)], axis=0)
    mod = _modulation(c8, mod_w, mod_b).reshape(depth, SUBLANES, N_MOD, 1, D)

    xl = x.reshape(B * L, D)
    xc = ctx.reshape(B * Lc, D)
    row = lambda v: v.reshape(1, -1)
    w_up = ffn_w_up.astype(bf16)
    w_dn = ffn_w_down.astype(bf16)
    cw = ffn_conv_w.reshape(depth, 9, F2)
    cb = ffn_conv_b.reshape(depth, 1, F2)

    for i in range(depth):
        kind = i % N_MIXERS
        j = i // N_MIXERS
        ctx_later = any(k % N_MIXERS == 2 for k in range(i + 1, depth))
        ctx_mix = ctx_later or kind == 2
        m_lat = [mod[i, 0:B, q] for q in range(N_MOD)]
        m_ctx = [mod[i, B:B + 1, q] for q in range(N_MOD)]
        streams = [("lat", xl, L, m_lat, True)]
        if ctx_mix:
            streams.append(("ctx", xc, Lc, m_ctx, ctx_later))
        g1n = row(norm1_g[i])
        g2n = row(norm2_g[i])
        new = {}
        if kind == 0:
            pw = pool_w[j].astype(bf16)
            for nm, xs, Ls, m, full in streams:
                if full:
                    new[nm] = _pool_layer(xs, g1n, m[1], m[0], m[2], pw, row(pool_scale[j]), L=Ls,
                                          name=f"pool{i}_{nm}")
        elif kind == 1:
            w1 = cv_w1[j].astype(bf16)
            w2 = cv_w2[j].astype(bf16)
            b1 = row(cv_b1[j])
            for nm, xs, Ls, m, full in streams:
                if full:
                    v = _prenorm_dual(xs, g1n, m[1], m[0], w1, w1, b1, b1, mode="glu", L=Ls,
                                      wb_off=D // min(PROJ_COLS, D), name=f"cv_glu{i}_{nm}")
                    new[nm] = _cconv_layer(v, cv_dw[j], row(cv_dw_b[j]), row(cv_ln_g[j]), row(cv_ln_b[j]), w2,
                                           row(cv_b2[j]), xs, m[2], L=Ls, name=f"cv_tail{i}_{nm}")
        else:
            wx = lru_w_x[j].astype(bf16)
            wy = lru_w_y[j].astype(bf16)
            wo = lru_w_out[j].astype(bf16)
            wg = jnp.concatenate([lru_wa[j], lru_wi[j]], axis=-1).astype(bf16)
            h0 = [jnp.zeros((B, 1, D), f32)] * 2
            for nm, xs, Ls, m, full in reversed(streams):
                xw, gy = _prenorm_dual(xs, g1n, m[1], m[0], wx, wy, None, None, mode="lru", L=Ls,
                                       wb_off=0, name=f"lru_in{i}_{nm}")
                s, ht = [], []
                for dr in range(2):
                    sd, hd = _lru_scan(xw, lru_conv_w[j], row(lru_conv_b[j]), wg[dr], row(lru_ba[j, dr]),
                                       row(lru_bi[j, dr]), row(lru_lambda[j, dr]), h0[dr], L=Ls,
                                       reverse=bool(dr), name=f"lru_scan{i}_{nm}{dr}")
                    s.append(sd)
                    ht.append(hd)
                h0 = ht
                if full:
                    new[nm] = _lru_out(s[0], s[1], gy, wo, xs, m[2], L=Ls, name=f"lru_out{i}_{nm}")
        fin = row(final_g) if i == depth - 1 else None
        xl = _ffn_layer(new["lat"], g2n, m_lat[4], m_lat[3], m_lat[5], w_up, cw, cb, w_dn, fin, layer=i, L=L,
                        two_d=True, name=f"ffn{i}_lat")
        if ctx_later:
            xc = _ffn_layer(new["ctx"], g2n, m_ctx[4], m_ctx[3], m_ctx[5], w_up, cw, cb, w_dn, None, layer=i,
                            L=Lc, two_d=False, name=f"ffn{i}_ctx")
    return xl.reshape(B, L, D)
```
